```python
import jax, jax.numpy as jnp
from jax import lax
import numpy as np

D_MODEL = 1024
BATCH = 4
SEQ = 4096
DEPTH = 2

N_MIXERS = 2
CHUNK = 64
RMS_EPS = 1e-6

RWKV_HEAD = 64
RWKV_HEADS = D_MODEL // RWKV_HEAD
DECAY_LORA = max(32, int(round(1.8 * D_MODEL ** 0.5 / 32)) * 32)
A_LORA = max(32, int(round(1.8 * D_MODEL ** 0.5 / 32)) * 32)
GATE_LORA = max(32, int(round(0.6 * D_MODEL ** 0.8 / 32)) * 32)
GN_EPS = 64e-5
N_SHIFT_MIX = 6

SGU_BLOCK = 128
D_SGU = D_MODEL
SGU_GROUPS = 8
SGU_GROUP_DIM = D_SGU // SGU_GROUPS
LN_EPS = 1e-5

D_FF = -(-8 * D_MODEL // (3 * 256)) * 256

N_A = (DEPTH + 1) // N_MIXERS + (0 if N_MIXERS == 2 else 0)
N_B = DEPTH // N_MIXERS

kernel_name = "rwkv7_sgu_interleaved_sandwich_trunk"


def _rmsnorm(x, g):
    x32 = x.astype(jnp.float32)
    y = x32 * lax.rsqrt(jnp.mean(x32 * x32, axis=-1, keepdims=True) + RMS_EPS)
    return (y * g.astype(jnp.float32)).astype(x.dtype)


def _token_shift(x):
    return jnp.pad(x[:, :-1], ((0, 0), (1, 0), (0, 0)))


def _rwkv7_scan(r, w, k, v, a, b):
    B, T, H, N = r.shape

    def step(S, inp):
        r_t, w_t, k_t, v_t, a_t, b_t = inp
        sa = jnp.einsum('bhij,bhj->bhi', S, a_t)
        S = (S * w_t[:, :, None, :]
             + sa[..., None] * b_t[:, :, None, :]
             + v_t[..., None] * k_t[:, :, None, :])
        y = jnp.einsum('bhij,bhj->bhi', S, r_t)
        return S, y

    xs = tuple(jnp.moveaxis(t, 1, 0) for t in (r, w, k, v, a, b))
    S0 = jnp.zeros((B, H, N, N), jnp.float32)
    _, y = lax.scan(step, S0, xs)
    return jnp.moveaxis(y, 0, 1)


def _rwkv7_time_mix(x, mix, w_rkv, w0, w1, w2, a0, a1, a2, g1, g2,
                    k_k, k_a, r_k, gn_w, gn_b, w_o):
    B, T, D = x.shape
    H, N = RWKV_HEADS, RWKV_HEAD
    f32 = jnp.float32
    xx = _token_shift(x) - x
    xs = x[None] + xx[None] * mix[:, None, None, :]
    r, k, v = jnp.einsum('cbtd,cde->cbte', xs[:3], w_rkv)
    xw, xa, xg = xs[3], xs[4], xs[5]
    w_raw = (w0 + jnp.tanh(xw @ w1) @ w2).astype(f32)
    w_log = -jax.nn.softplus(-w_raw) - 0.5
    decay = jnp.exp(-jnp.exp(w_log))
    a = jax.nn.sigmoid(a0 + (xa @ a1) @ a2)
    g = jax.nn.sigmoid(xg @ g1) @ g2
    heads = lambda t: t.reshape(B, T, H, N).astype(f32)
    kk = heads(k * k_k)
    kk = kk / jnp.maximum(jnp.sqrt(jnp.sum(kk * kk, axis=-1, keepdims=True)), 1e-12)
    k = k * (1.0 + (a - 1.0) * k_a)
    r_h, k_h, v_h, a_h = heads(r), heads(k), heads(v), heads(a)
    y = _rwkv7_scan(r_h, decay.reshape(B, T, H, N), k_h, v_h, -kk, kk * a_h)
    mu = jnp.mean(y, axis=-1, keepdims=True)
    var = jnp.mean(jnp.square(y - mu), axis=-1, keepdims=True)
    y = (y - mu) * lax.rsqrt(var + GN_EPS)
    y = y * gn_w.reshape(H, N).astype(f32) + gn_b.reshape(H, N).astype(f32)
    y = y + jnp.sum(r_h * k_h * r_k.astype(f32), axis=-1, keepdims=True) * v_h
    y = y.reshape(B, T, D).astype(x.dtype)
    return (y * g) @ w_o


def _sgu_mixer(x, w_in, b_in, ln_w, ln_b, ws, bs, w_out):
    B, T, D = x.shape
    h = jax.nn.gelu(x @ w_in + b_in, approximate=False)
    u, v = jnp.split(h, 2, axis=-1)
    v32 = v.astype(jnp.float32)
    mu = jnp.mean(v32, axis=-1, keepdims=True)
    var = jnp.mean(jnp.square(v32 - mu), axis=-1, keepdims=True)
    v = ((v32 - mu) * lax.rsqrt(var + LN_EPS) * ln_w.astype(jnp.float32)
         + ln_b.astype(jnp.float32)).astype(x.dtype)
    n_blk = T // SGU_BLOCK
    vb = v.reshape(B, n_blk, SGU_BLOCK, SGU_GROUPS, SGU_GROUP_DIM)
    chunk_id = jnp.arange(SGU_BLOCK) // CHUNK
    mask = chunk_id[None, :] <= chunk_id[:, None]
    ws_m = jnp.where(mask[None], ws, jnp.zeros((), ws.dtype))
    s = jnp.einsum('gij,bnjgc->bnigc', ws_m, vb) + bs.T[None, None, :, :, None]
    s = s.reshape(B, T, D_SGU)
    return (u * s) @ w_out


def _swiglu(x, w_gate, w_up, w_down):
    return (jax.nn.silu(x @ w_gate) * (x @ w_up)) @ w_down


def setup_inputs(seed: int = 0) -> dict:
    key = jax.random.key(seed)
    ks = jax.random.split(key, 32)
    D = D_MODEL
    nrm = lambda k, shape, scale: jax.random.normal(k, shape, jnp.float32) * scale
    return {
        "x": nrm(ks[0], (BATCH, SEQ, D), 1.0),
        "norm_gains": 1.0 + nrm(ks[1], (DEPTH, 4, D), 0.1),
        "rwkv_mix": jax.random.uniform(ks[2], (N_A, N_SHIFT_MIX, D), jnp.float32),
        "rwkv_w_rkv": nrm(ks[3], (N_A, 3, D, D), D ** -0.5),
        "rwkv_w0": jax.random.uniform(ks[4], (N_A, D), jnp.float32, -6.5, -1.5),
        "rwkv_w1": nrm(ks[5], (N_A, D, DECAY_LORA), D ** -0.5),
        "rwkv_w2": nrm(ks[6], (N_A, DECAY_LORA, D), 0.5 * DECAY_LORA ** -0.5),
        "rwkv_a0": nrm(ks[7], (N_A, D), 0.1),
        "rwkv_a1": nrm(ks[8], (N_A, D, A_LORA), D ** -0.5),
        "rwkv_a2": nrm(ks[9], (N_A, A_LORA, D), 0.5 * A_LORA ** -0.5),
        "rwkv_g1": nrm(ks[10], (N_A, D, GATE_LORA), D ** -0.5),
        "rwkv_g2": nrm(ks[11], (N_A, GATE_LORA, D), GATE_LORA ** -0.5),
        "rwkv_k_k": 0.85 + nrm(ks[12], (N_A, D), 0.05),
        "rwkv_k_a": 1.0 + nrm(ks[13], (N_A, D), 0.05),
        "rwkv_r_k": -0.04 + nrm(ks[14], (N_A, RWKV_HEADS, RWKV_HEAD), 0.02),
        "rwkv_gn_w": 1.0 + nrm(ks[15], (N_A, D), 0.1),
        "rwkv_gn_b": nrm(ks[16], (N_A, D), 0.02),
        "rwkv_w_o": nrm(ks[17], (N_A, D, D), D ** -0.5),
        "sgu_w_in": nrm(ks[18], (N_B, D, 2 * D_SGU), D ** -0.5),
        "sgu_b_in": nrm(ks[19], (N_B, 2 * D_SGU), 0.02),
        "sgu_ln_w": 1.0 + nrm(ks[20], (N_B, D_SGU), 0.1),
        "sgu_ln_b": nrm(ks[21], (N_B, D_SGU), 0.02),
        "sgu_ws": nrm(ks[22], (N_B, SGU_GROUPS, SGU_BLOCK, SGU_BLOCK), SGU_BLOCK ** -0.5),
        "sgu_bs": 1.0 + nrm(ks[23], (N_B, SGU_GROUPS, SGU_BLOCK), 0.1),
        "sgu_w_out": nrm(ks[24], (N_B, D_SGU, D), D_SGU ** -0.5),
        "ffn_w_gate": nrm(ks[25], (DEPTH, D, D_FF), D ** -0.5),
        "ffn_w_up": nrm(ks[26], (DEPTH, D, D_FF), D ** -0.5),
        "ffn_w_down": nrm(ks[27], (DEPTH, D_FF, D), D_FF ** -0.5),
    }


def reference(x, norm_gains,
              rwkv_mix, rwkv_w_rkv, rwkv_w0, rwkv_w1, rwkv_w2, rwkv_a0, rwkv_a1, rwkv_a2,
              rwkv_g1, rwkv_g2, rwkv_k_k, rwkv_k_a, rwkv_r_k, rwkv_gn_w, rwkv_gn_b, rwkv_w_o,
              sgu_w_in, sgu_b_in, sgu_ln_w, sgu_ln_b, sgu_ws, sgu_bs, sgu_w_out,
              ffn_w_gate, ffn_w_up, ffn_w_down):
    for i in range(DEPTH):
        j = i // N_MIXERS
        h = _rmsnorm(x, norm_gains[i, 0])
        if i % N_MIXERS == 0:
            h = _rwkv7_time_mix(h, rwkv_mix[j], rwkv_w_rkv[j], rwkv_w0[j], rwkv_w1[j],
                                rwkv_w2[j], rwkv_a0[j], rwkv_a1[j], rwkv_a2[j], rwkv_g1[j],
                                rwkv_g2[j], rwkv_k_k[j], rwkv_k_a[j], rwkv_r_k[j],
                                rwkv_gn_w[j], rwkv_gn_b[j], rwkv_w_o[j])
        else:
            h = _sgu_mixer(h, sgu_w_in[j], sgu_b_in[j], sgu_ln_w[j], sgu_ln_b[j],
                           sgu_ws[j], sgu_bs[j], sgu_w_out[j])
        x = x + _rmsnorm(h, norm_gains[i, 1])
        h = _rmsnorm(x, norm_gains[i, 2])
        h = _swiglu(h, ffn_w_gate[i], ffn_w_up[i], ffn_w_down[i])
        x = x + _rmsnorm(h, norm_gains[i, 3])
    return x
```

```python
import functools
import math

import jax
import jax.numpy as jnp
from jax import lax
from jax.experimental import pallas as pl
from jax.experimental.pallas import tpu as pltpu

F32 = jnp.float32
BF16 = jnp.bfloat16

D_MODEL = 1024
HEAD = 64
PAIR = 2 * HEAD
N_PAIRS = D_MODEL // PAIR
CHUNK = 64
SGU_BLOCK = 128
SGU_GROUPS = 8
D_FF = 2816
RMS_EPS = 1e-6
GN_EPS = 64e-5
LN_EPS = 1e-5

PRE_ROWS = 256
SCAN_ROWS = 512
FFN_ROWS = 512
FFN_COLS = 1408
VMEM_LIMIT = 56 * 1024 * 1024


def _dot(a, b):
    return jnp.dot(a.astype(BF16), b.astype(BF16), preferred_element_type=F32)


def _dot_nt(a, b):
    return lax.dot_general(a.astype(BF16), b.astype(BF16), (((1,), (1,)), ((), ())),
                           preferred_element_type=F32)


def _split2(x):
    hi = x.astype(BF16)
    lo = (x - hi.astype(F32)).astype(BF16)
    return hi, lo


def _split3(x):
    hi = x.astype(BF16)
    r1 = x - hi.astype(F32)
    mid = r1.astype(BF16)
    lo = (r1 - mid.astype(F32)).astype(BF16)
    return hi, mid, lo


def _rmsnorm(x, g):
    return x * lax.rsqrt(jnp.mean(x * x, axis=-1, keepdims=True) + RMS_EPS) * g


def _softplus(z):
    return jnp.maximum(z, 0.0) + jnp.log1p(jnp.exp(-jnp.abs(z)))


def _gelu(x):
    return x * (lax.erf(x * (1.0 / math.sqrt(2.0))) + 1.0) * 0.5


def _ffn(h, wg_ref, wu_ref, wd_ref):
    hb = h.astype(BF16)
    acc = None
    for j in range(D_FF // FFN_COLS):
        sl = slice(j * FFN_COLS, (j + 1) * FFN_COLS)
        gate = _dot(hb, wg_ref[:, sl])
        up = _dot(hb, wu_ref[:, sl])
        part = _dot(gate * jax.nn.sigmoid(gate) * up, wd_ref[sl, :])
        acc = part if acc is None else acc + part
    return acc


def _rwkv_pre_kernel(x_ref, xp_ref, gain_ref, mix_ref, wrkv_ref, w0_ref, w1_ref, w2_ref,
                     a0_ref, a1_ref, a2_ref, g1_ref, g2_ref, kk_ref, ka_ref, rk_ref,
                     e_ref, et_ref, tri_ref, sel_ref,
                     rt_ref, at_ref, kt_ref, bt_ref, kh_ref, bh_ref, v_ref, bonus_ref,
                     g_ref, gl_ref):
    i = pl.program_id(1)
    gain = gain_ref[...]
    h = _rmsnorm(x_ref[0], gain)
    hp = _rmsnorm(xp_ref[0], gain)[7:8]
    hp = jnp.where(i > 0, hp, 0.0)
    row = lax.broadcasted_iota(jnp.int32, h.shape, 0)
    h_prev = jnp.where(row == 0, hp, pltpu.roll(h, 1, 0))
    xx = h_prev - h
    mix = mix_ref[...]

    def shifted(c):
        return (h + xx * mix[c:c + 1]).astype(BF16)

    r = _dot(shifted(0), wrkv_ref[0])
    k = _dot(shifted(1), wrkv_ref[1])
    v = _dot(shifted(2), wrkv_ref[2])
    w_raw = w0_ref[...] + _dot(jnp.tanh(_dot(shifted(3), w1_ref[...])), w2_ref[...])
    lw = -jnp.exp(-_softplus(-w_raw) - 0.5)
    a = jax.nn.sigmoid(a0_ref[...] + _dot(_dot(shifted(4), a1_ref[...]), a2_ref[...]))
    g_ref[0] = _dot(jax.nn.sigmoid(_dot(shifted(5), g1_ref[...])), g2_ref[...])

    def head_sum(q_parts):
        s = None
        for q in q_parts:
            t = _dot(q, e_ref[...])
            s = t if s is None else s + t
        s_hi, s_lo = _split2(s)
        return _dot(s_hi, et_ref[...]) + _dot(s_lo, et_ref[...])

    kkr = k * kk_ref[...]
    ss = head_sum(_split2(kkr * kkr))
    kk = kkr / jnp.maximum(jnp.sqrt(ss), 1e-12)
    k2 = k * (1.0 + (a - 1.0) * ka_ref[...])
    bonus = head_sum(_split2(r * k2 * rk_ref[...])) * v
    avec = -kk
    bvec = kk * a

    parts = _split3(lw)
    c = sum(_dot(tri_ref[...], p) for p in parts)
    tot = sum(_dot(sel_ref[...], p) for p in parts)
    n_chunks = h.shape[0] // CHUNK
    c_last = jnp.concatenate(
        [jnp.broadcast_to(tot[j:j + 1], (CHUNK, D_MODEL)) for j in range(n_chunks)], axis=0)
    e_neg = jnp.exp(-c)
    e_rest = jnp.exp(c_last - c)
    outs = (
        (rt_ref, r * jnp.exp(c)),
        (at_ref, avec * jnp.exp(c - lw)),
        (kt_ref, k2 * e_neg),
        (bt_ref, bvec * e_neg),
        (kh_ref, k2 * e_rest),
        (bh_ref, bvec * e_rest),
        (v_ref, v),
        (bonus_ref, bonus),
    )
    gl = jnp.exp(tot[:n_chunks])
    for p in range(N_PAIRS):
        sl = slice(p * PAIR, (p + 1) * PAIR)
        for ref, val in outs:
            ref[0, p] = val[:, sl].astype(ref.dtype)
        for j in range(n_chunks):
            gl_ref[0, p, j] = gl[j:j + 1, sl]


def _rwkv_scan_kernel(rt_ref, at_ref, kt_ref, bt_ref, kh_ref, bh_ref, v_ref, bonus_ref, gl_ref,
                      gnw_ref, gnb_ref, ones_ref, y_ref, s_ref):
    n_batch = rt_ref.shape[0]
    rows = rt_ref.shape[2]

    @pl.when(pl.program_id(1) == 0)
    def _():
        s_ref[...] = jnp.zeros_like(s_ref)

    lane = lax.broadcasted_iota(jnp.int32, (CHUNK, PAIR), 1)
    trow = lax.broadcasted_iota(jnp.int32, (CHUNK, PAIR), 0)
    tcol = lane & (HEAD - 1)
    first_head = lane < HEAD
    strict = tcol < trow
    incl = tcol <= trow
    same16 = (tcol >> 4) == (trow >> 4)
    same32 = (tcol >> 5) == (trow >> 5)
    eye = jnp.where(tcol == trow, 1.0, 0.0).astype(F32)
    r2 = lax.broadcasted_iota(jnp.int32, (PAIR, PAIR), 0)
    c2 = lax.broadcasted_iota(jnp.int32, (PAIR, PAIR), 1)
    same_head = (r2 >> 6) == (c2 >> 6)

    def bd(y):
        z = jnp.zeros_like(y)
        return jnp.concatenate([jnp.where(first_head, y, z), jnp.where(first_head, z, y)], axis=0)

    def pdot(x, y):
        return _dot(x, bd(y.astype(BF16)))

    def chunk(ci, carry):
        r0 = pl.multiple_of(ci * CHUNK, CHUNK)
        sl = pl.ds(r0, CHUNK)
        for b in range(n_batch):
            rt = rt_ref[b, 0, sl, :]
            at = at_ref[b, 0, sl, :]
            kt = kt_ref[b, 0, sl, :]
            bt = bt_ref[b, 0, sl, :]
            vv = v_ref[b, 0, sl, :]
            p_all = _dot_nt(jnp.concatenate([at, rt], axis=0),
                            jnp.concatenate([bd(bt), bd(kt)], axis=0))
            a_ab = jnp.where(strict, p_all[:CHUNK, :PAIR], 0.0)
            a_ak = jnp.where(strict, p_all[:CHUNK, PAIR:], 0.0)
            a_rb = jnp.where(incl, p_all[CHUNK:, :PAIR], 0.0)
            a_rk = jnp.where(incl, p_all[CHUNK:, PAIR:], 0.0)

            a_d = jnp.where(same16, a_ab, 0.0)
            a_o1 = jnp.where(jnp.logical_and(same32, jnp.logical_not(same16)), a_ab, 0.0)
            a_o2 = jnp.where(same32, 0.0, a_ab)
            t_m = eye + a_d
            p_m = pdot(a_d, a_d)
            for _ in range(2):
                res = pdot(jnp.concatenate([p_m, t_m], axis=0), p_m)
                t_m = t_m + res[CHUNK:]
                p_m = res[:CHUNK]
            t_m = t_m + pdot(t_m, p_m)
            t_m = t_m + pdot(pdot(t_m, a_o1), t_m)
            t_m = t_m + pdot(pdot(t_m, a_o2), t_m)

            av = _dot(jnp.concatenate([a_ak, a_rk], axis=0), bd(vv))
            tx = _dot(t_m, jnp.concatenate([bd(at), bd(av[:CHUNK].astype(BF16))], axis=1))
            at2 = tx[:, :PAIR]
            w_u = tx[:, PAIR:]

            s_old = s_ref[b]
            q = _dot_nt(jnp.concatenate([at2.astype(BF16), rt], axis=0), s_old)
            u = q[:CHUNK] + w_u
            y = q[CHUNK:] + av[CHUNK:] + pdot(a_rb, u)
            y_ref[b, 0, sl, :] = y
            uv_t = jnp.concatenate([u, vv.astype(F32)], axis=0).T
            bk = jnp.concatenate([bh_ref[b, 0, sl, :], kh_ref[b, 0, sl, :]], axis=0)
            d_s = _dot(uv_t, bk)
            s_ref[b] = s_old * gl_ref[b, 0, ci] + jnp.where(same_head, d_s, 0.0)
        return carry

    lax.fori_loop(0, rows // CHUNK, chunk, 0)

    y = y_ref[:, 0].reshape(n_batch * rows, PAIR)
    ones = ones_ref[...]

    def head_mean(z):
        hi, lo = _split2(z)
        return (_dot(hi, ones) + _dot(lo, ones)) * (1.0 / HEAD)

    d = y - head_mean(y)
    yn = d * lax.rsqrt(head_mean(d * d) + GN_EPS) * gnw_ref[0] + gnb_ref[0]
    yn = yn.reshape(n_batch, rows, PAIR) + bonus_ref[:, 0]
    y_ref[:, 0] = yn


def _mix_ffn_kernel(x_ref, y_ref, g_ref, wo_ref, gains_ref, wg_ref, wu_ref, wd_ref, o_ref):
    y = jnp.concatenate([y_ref[0, p] for p in range(N_PAIRS)], axis=-1)
    m = _dot(y * g_ref[0], wo_ref[...])
    x1 = x_ref[0] + _rmsnorm(m, gains_ref[0:1])
    f = _ffn(_rmsnorm(x1, gains_ref[1:2]), wg_ref, wu_ref, wd_ref)
    o_ref[0] = x1 + _rmsnorm(f, gains_ref[2:3])


def _sgu_ffn_kernel(x_ref, gains_ref, win_ref, bin_ref, lnw_ref, lnb_ref, ws_ref, bs_ref,
                    wout_ref, wg_ref, wu_ref, wd_ref, o_ref):
    x = x_ref[0]
    rows = x.shape[0]
    hb = _rmsnorm(x, gains_ref[0:1]).astype(BF16)
    u = _gelu(_dot(hb, win_ref[:, :D_MODEL]) + bin_ref[:, :D_MODEL])
    v = _gelu(_dot(hb, win_ref[:, D_MODEL:]) + bin_ref[:, D_MODEL:])
    mu = jnp.mean(v, axis=-1, keepdims=True)
    dv = v - mu
    var = jnp.mean(dv * dv, axis=-1, keepdims=True)
    vn = (dv * lax.rsqrt(var + LN_EPS) * lnw_ref[...] + lnb_ref[...]).astype(BF16)

    ri = lax.broadcasted_iota(jnp.int32, (SGU_BLOCK, SGU_BLOCK), 0)
    ci = lax.broadcasted_iota(jnp.int32, (SGU_BLOCK, SGU_BLOCK), 1)
    causal = (ci // CHUNK) <= (ri // CHUNK)
    cols = []
    for gi in range(SGU_GROUPS):
        ws_m = jnp.where(causal, ws_ref[gi], 0.0).astype(BF16)
        lanes = slice(gi * SGU_BLOCK, (gi + 1) * SGU_BLOCK)
        blocks = [_dot(ws_m, vn[n * SGU_BLOCK:(n + 1) * SGU_BLOCK, lanes]) + bs_ref[:, lanes]
                  for n in range(rows // SGU_BLOCK)]
        cols.append(jnp.concatenate(blocks, axis=0))
    s = jnp.concatenate(cols, axis=-1)
    m = _dot(u * s, wout_ref[...])
    x1 = x + _rmsnorm(m, gains_ref[1:2])
    f = _ffn(_rmsnorm(x1, gains_ref[2:3]), wg_ref, wu_ref, wd_ref)
    o_ref[0] = x1 + _rmsnorm(f, gains_ref[3:4])


def _resident(shape):
    zeros = (0,) * len(shape)
    return pl.BlockSpec(shape, lambda *_: zeros, pipeline_mode=pl.Buffered(1))


def kernel(x, norm_gains, rwkv_mix, rwkv_w_rkv, rwkv_w0, rwkv_w1, rwkv_w2, rwkv_a0, rwkv_a1, rwkv_a2, rwkv_g1, rwkv_g2, rwkv_k_k, rwkv_k_a, rwkv_r_k, rwkv_gn_w, rwkv_gn_b, rwkv_w_o, sgu_w_in, sgu_b_in, sgu_ln_w, sgu_ln_b, sgu_ws, sgu_bs, sgu_w_out, ffn_w_gate, ffn_w_up, ffn_w_down):
    n_b, n_t, d = x.shape
    assert d == D_MODEL and n_t % SCAN_ROWS == 0 and n_t % FFN_ROWS == 0 and n_t % PRE_ROWS == 0
    row2 = lambda p: p.reshape(1, -1)

    ch = jnp.arange(D_MODEL) // HEAD
    e_mat = (ch[:, None] == jnp.arange(PAIR)[None, :]).astype(BF16)
    et_mat = e_mat.T
    tt = jnp.arange(PRE_ROWS)
    tri = jnp.logical_and(tt[:, None] // CHUNK == tt[None, :] // CHUNK,
                          tt[None, :] <= tt[:, None]).astype(BF16)
    sel = (jnp.arange(8)[:, None] == tt[None, :] // CHUNK).astype(BF16)
    pl_idx = jnp.arange(PAIR) // HEAD
    ones_blk = (pl_idx[:, None] == pl_idx[None, :]).astype(BF16)

    n_pre = n_t // PRE_ROWS
    slab = jax.ShapeDtypeStruct((n_b, N_PAIRS, n_t, PAIR), BF16)
    slab_spec = pl.BlockSpec((1, N_PAIRS, PRE_ROWS, PAIR), lambda b, i: (b, 0, i, 0))
    pre_out = pl.pallas_call(
        _rwkv_pre_kernel,
        grid=(n_b, n_pre),
        in_specs=[
            pl.BlockSpec((1, PRE_ROWS, D_MODEL), lambda b, i: (b, i, 0)),
            pl.BlockSpec((1, 8, D_MODEL), lambda b, i: (b, jnp.maximum(i * (PRE_ROWS // 8) - 1, 0), 0)),
            _resident((1, D_MODEL)),
            _resident((6, D_MODEL)),
            _resident((3, D_MODEL, D_MODEL)),
            _resident((1, D_MODEL)), _resident(rwkv_w1.shape[1:]), _resident(rwkv_w2.shape[1:]),
            _resident((1, D_MODEL)), _resident(rwkv_a1.shape[1:]), _resident(rwkv_a2.shape[1:]),
            _resident(rwkv_g1.shape[1:]), _resident(rwkv_g2.shape[1:]),
            _resident((1, D_MODEL)), _resident((1, D_MODEL)), _resident((1, D_MODEL)),
            _resident(e_mat.shape), _resident(et_mat.shape), _resident(tri.shape), _resident(sel.shape),
        ],
        out_specs=[slab_spec] * 7 + [
            slab_spec,
            pl.BlockSpec((1, PRE_ROWS, D_MODEL), lambda b, i: (b, i, 0)),
            pl.BlockSpec((1, N_PAIRS, PRE_ROWS // CHUNK, 1, PAIR), lambda b, i: (b, 0, i, 0, 0)),
        ],
        out_shape=[slab] * 7 + [
            jax.ShapeDtypeStruct((n_b, N_PAIRS, n_t, PAIR), F32),
            jax.ShapeDtypeStruct((n_b, n_t, D_MODEL), F32),
            jax.ShapeDtypeStruct((n_b, N_PAIRS, n_t // CHUNK, 1, PAIR), F32),
        ],
        compiler_params=pltpu.CompilerParams(
            dimension_semantics=("arbitrary", "arbitrary"), vmem_limit_bytes=VMEM_LIMIT),
        name="rwkv_pre",
    )(x, x, row2(norm_gains[0, 0]), rwkv_mix[0], rwkv_w_rkv[0].astype(BF16),
      row2(rwkv_w0[0]), rwkv_w1[0].astype(BF16), rwkv_w2[0].astype(BF16),
      row2(rwkv_a0[0]), rwkv_a1[0].astype(BF16), rwkv_a2[0].astype(BF16),
      rwkv_g1[0].astype(BF16), rwkv_g2[0].astype(BF16),
      row2(rwkv_k_k[0]), row2(rwkv_k_a[0]), row2(rwkv_r_k[0]),
      e_mat, et_mat, tri, sel)
    rt, at, kt, bt, kh, bh, vv, bonus, gate, gl = pre_out

    scan_spec = pl.BlockSpec((n_b, 1, SCAN_ROWS, PAIR), lambda p, t: (0, p, t, 0))
    y = pl.pallas_call(
        _rwkv_scan_kernel,
        grid=(N_PAIRS, n_t // SCAN_ROWS),
        in_specs=[scan_spec] * 8 + [
            pl.BlockSpec((n_b, 1, SCAN_ROWS // CHUNK, 1, PAIR), lambda p, t: (0, p, t, 0, 0)),
            pl.BlockSpec((1, 1, PAIR), lambda p, t: (p, 0, 0)),
            pl.BlockSpec((1, 1, PAIR), lambda p, t: (p, 0, 0)),
            _resident(ones_blk.shape),
        ],
        out_specs=scan_spec,
        out_shape=jax.ShapeDtypeStruct((n_b, N_PAIRS, n_t, PAIR), F32),
        scratch_shapes=[pltpu.VMEM((n_b, PAIR, PAIR), F32)],
        compiler_params=pltpu.CompilerParams(
            dimension_semantics=("arbitrary", "arbitrary"), vmem_limit_bytes=VMEM_LIMIT),
        name="rwkv_scan",
    )(rt, at, kt, bt, kh, bh, vv, bonus, gl,
      rwkv_gn_w[0].reshape(N_PAIRS, 1, PAIR), rwkv_gn_b[0].reshape(N_PAIRS, 1, PAIR), ones_blk)

    n_ffn = n_t // FFN_ROWS
    row_spec = pl.BlockSpec((1, FFN_ROWS, D_MODEL), lambda b, i: (b, i, 0))
    ffn_specs = [_resident((D_MODEL, D_FF)), _resident((D_MODEL, D_FF)), _resident((D_FF, D_MODEL))]
    x = pl.pallas_call(
        _mix_ffn_kernel,
        grid=(n_b, n_ffn),
        in_specs=[
            row_spec,
            pl.BlockSpec((1, N_PAIRS, FFN_ROWS, PAIR), lambda b, i: (b, 0, i, 0)),
            row_spec,
            _resident((D_MODEL, D_MODEL)),
            _resident((3, D_MODEL)),
        ] + ffn_specs,
        out_specs=row_spec,
        out_shape=jax.ShapeDtypeStruct((n_b, n_t, D_MODEL), F32),
        compiler_params=pltpu.CompilerParams(
            dimension_semantics=("arbitrary", "arbitrary"), vmem_limit_bytes=VMEM_LIMIT),
        name="mix_ffn",
    )(x, y, gate, rwkv_w_o[0].astype(BF16), norm_gains[0, 1:4],
      ffn_w_gate[0].astype(BF16), ffn_w_up[0].astype(BF16), ffn_w_down[0].astype(BF16))

    bs_full = jnp.repeat(sgu_bs[0].T, SGU_BLOCK, axis=1)
    x = pl.pallas_call(
        _sgu_ffn_kernel,
        grid=(n_b, n_ffn),
        in_specs=[
            row_spec,
            _resident((4, D_MODEL)),
            _resident((D_MODEL, 2 * D_MODEL)),
            _resident((1, 2 * D_MODEL)),
            _resident((1, D_MODEL)), _resident((1, D_MODEL)),
            _resident((SGU_GROUPS, SGU_BLOCK, SGU_BLOCK)),
            _resident((SGU_BLOCK, D_MODEL)),
            _resident((D_MODEL, D_MODEL)),
        ] + ffn_specs,
        out_specs=row_spec,
        out_shape=jax.ShapeDtypeStruct((n_b, n_t, D_MODEL), F32),
        compiler_params=pltpu.CompilerParams(
            dimension_semantics=("arbitrary", "arbitrary"), vmem_limit_bytes=VMEM_LIMIT),
        name="sgu_ffn",
    )(x, norm_gains[1], sgu_w_in[0].astype(BF16), row2(sgu_b_in[0]),
      row2(sgu_ln_w[0]), row2(sgu_ln_b[0]), sgu_ws[0], bs_full, sgu_w_out[0].astype(BF16),
      ffn_w_gate[1].astype(BF16), ffn_w_up[1].astype(BF16), ffn_w_down[1].astype(BF16))
    return x
```

```python
import functools
import math

import jax
import jax.numpy as jnp
from jax import lax
from jax.experimental import pallas as pl
from jax.experimental.pallas import tpu as pltpu

F32 = jnp.float32
BF16 = jnp.bfloat16

D_MODEL = 1024
HEAD = 64
PAIR = 2 * HEAD
N_PAIRS = D_MODEL // PAIR
CHUNK = 64
SGU_BLOCK = 128
SGU_GROUPS = 8
D_FF = 2816
RMS_EPS = 1e-6
GN_EPS = 64e-5
LN_EPS = 1e-5

PRE_ROWS = 256
SCAN_ROWS = 512
P1_CHUNKS = 4
FFN_ROWS = 512
FFN_COLS = 1408
VMEM_LIMIT = 56 * 1024 * 1024


def _dot(a, b):
    return jnp.dot(a.astype(BF16), b.astype(BF16), preferred_element_type=F32)


def _dot_nt(a, b):
    return lax.dot_general(a.astype(BF16), b.astype(BF16), (((1,), (1,)), ((), ())),
                           preferred_element_type=F32)


def _split2(x):
    hi = x.astype(BF16)
    lo = (x - hi.astype(F32)).astype(BF16)
    return hi, lo


def _split3(x):
    hi = x.astype(BF16)
    r1 = x - hi.astype(F32)
    mid = r1.astype(BF16)
    lo = (r1 - mid.astype(F32)).astype(BF16)
    return hi, mid, lo


def _rmsnorm(x, g):
    return x * lax.rsqrt(jnp.mean(x * x, axis=-1, keepdims=True) + RMS_EPS) * g


def _softplus(z):
    return jnp.maximum(z, 0.0) + jnp.log1p(jnp.exp(-jnp.abs(z)))


def _gelu(x):
    return x * (lax.erf(x * (1.0 / math.sqrt(2.0))) + 1.0) * 0.5


def _ffn(h, wg_ref, wu_ref, wd_ref):
    hb = h.astype(BF16)
    acc = None
    for j in range(D_FF // FFN_COLS):
        sl = slice(j * FFN_COLS, (j + 1) * FFN_COLS)
        gate = _dot(hb, wg_ref[:, sl])
        up = _dot(hb, wu_ref[:, sl])
        part = _dot(gate * jax.nn.sigmoid(gate) * up, wd_ref[sl, :])
        acc = part if acc is None else acc + part
    return acc


def _rwkv_pre_kernel(x_ref, xp_ref, gain_ref, mix_ref, wrkv_ref, w0_ref, w1_ref, w2_ref,
                     a0_ref, a1_ref, a2_ref, g1_ref, g2_ref, kk_ref, ka_ref, rk_ref,
                     e_ref, et_ref, tri_ref, sel_ref,
                     rt_ref, at_ref, kt_ref, bt_ref, kh_ref, bh_ref, v_ref, bonus_ref,
                     g_ref, gl_ref):
    i = pl.program_id(1)
    gain = gain_ref[...]
    h = _rmsnorm(x_ref[0], gain)
    hp = _rmsnorm(xp_ref[0], gain)[7:8]
    hp = jnp.where(i > 0, hp, 0.0)
    row = lax.broadcasted_iota(jnp.int32, h.shape, 0)
    h_prev = jnp.where(row == 0, hp, pltpu.roll(h, 1, 0))
    xx = h_prev - h
    mix = mix_ref[...]

    def shifted(c):
        return (h + xx * mix[c:c + 1]).astype(BF16)

    r = _dot(shifted(0), wrkv_ref[0])
    k = _dot(shifted(1), wrkv_ref[1])
    v = _dot(shifted(2), wrkv_ref[2])
    w_raw = w0_ref[...] + _dot(jnp.tanh(_dot(shifted(3), w1_ref[...])), w2_ref[...])
    lw = -jnp.exp(-_softplus(-w_raw) - 0.5)
    a = jax.nn.sigmoid(a0_ref[...] + _dot(_dot(shifted(4), a1_ref[...]), a2_ref[...]))
    g_ref[0] = _dot(jax.nn.sigmoid(_dot(shifted(5), g1_ref[...])), g2_ref[...])

    def head_sum(q_parts):
        s = None
        for q in q_parts:
            t = _dot(q, e_ref[...])
            s = t if s is None else s + t
        s_hi, s_lo = _split2(s)
        return _dot(s_hi, et_ref[...]) + _dot(s_lo, et_ref[...])

    kkr = k * kk_ref[...]
    ss = head_sum(_split2(kkr * kkr))
    kk = kkr / jnp.maximum(jnp.sqrt(ss), 1e-12)
    k2 = k * (1.0 + (a - 1.0) * ka_ref[...])
    bonus = head_sum(_split2(r * k2 * rk_ref[...])) * v
    avec = -kk
    bvec = kk * a

    parts = _split3(lw)
    c = sum(_dot(tri_ref[...], p) for p in parts)
    tot = sum(_dot(sel_ref[...], p) for p in parts)
    n_chunks = h.shape[0] // CHUNK
    c_last = jnp.concatenate(
        [jnp.broadcast_to(tot[j:j + 1], (CHUNK, D_MODEL)) for j in range(n_chunks)], axis=0)
    e_neg = jnp.exp(-c)
    e_rest = jnp.exp(c_last - c)
    outs = (
        (rt_ref, r * jnp.exp(c)),
        (at_ref, avec * jnp.exp(c - lw)),
        (kt_ref, k2 * e_neg),
        (bt_ref, bvec * e_neg),
        (kh_ref, k2 * e_rest),
        (bh_ref, bvec * e_rest),
        (v_ref, v),
        (bonus_ref, bonus),
    )
    gl = jnp.exp(tot[:n_chunks])
    for p in range(N_PAIRS):
        sl = slice(p * PAIR, (p + 1) * PAIR)
        for ref, val in outs:
            ref[0, p] = val[:, sl].astype(ref.dtype)
        for j in range(n_chunks):
            gl_ref[0, p, j] = gl[j:j + 1, sl]


def _rwkv_scan_kernel(rt_ref, at_ref, kt_ref, bt_ref, kh_ref, bh_ref, v_ref, bonus_ref, gl_ref,
                      gnw_ref, gnb_ref, ones_ref, y_ref, s_ref, r2_ref, y0_ref, mlr_ref, nc_ref):
    n_batch = rt_ref.shape[0]
    rows = rt_ref.shape[2]

    @pl.when(pl.program_id(1) == 0)
    def _():
        s_ref[...] = jnp.zeros_like(s_ref)

    lane = lax.broadcasted_iota(jnp.int32, (CHUNK, PAIR), 1)
    trow = lax.broadcasted_iota(jnp.int32, (CHUNK, PAIR), 0)
    tcol = lane & (HEAD - 1)
    first_head = lane < HEAD
    strict = tcol < trow
    incl = tcol <= trow
    same16 = (tcol >> 4) == (trow >> 4)
    same32 = (tcol >> 5) == (trow >> 5)
    eye = jnp.where(tcol == trow, 1.0, 0.0).astype(F32)
    r2 = lax.broadcasted_iota(jnp.int32, (PAIR, PAIR), 0)
    c2 = lax.broadcasted_iota(jnp.int32, (PAIR, PAIR), 1)
    same_head = (r2 >> 6) == (c2 >> 6)

    def bd(y):
        z = jnp.zeros_like(y)
        return jnp.concatenate([jnp.where(first_head, y, z), jnp.where(first_head, z, y)], axis=0)

    def pdot(x, y):
        return _dot(x, bd(y.astype(BF16)))

    cat = jnp.concatenate

    def phase1(it, carry):
        where = [(b, pl.ds(pl.multiple_of((it * P1_CHUNKS + cj) * CHUNK, CHUNK), CHUNK))
                 for cj in range(P1_CHUNKS) for b in range(n_batch)]
        rt = [rt_ref[b, 0, sl, :] for b, sl in where]
        at = [at_ref[b, 0, sl, :] for b, sl in where]
        kt = [kt_ref[b, 0, sl, :] for b, sl in where]
        bt = [bt_ref[b, 0, sl, :] for b, sl in where]
        vv = [v_ref[b, 0, sl, :] for b, sl in where]
        n = len(where)
        idx = range(n)
        p_all = [_dot_nt(cat([at[i], rt[i]], axis=0), cat([bd(bt[i]), bd(kt[i])], axis=0)) for i in idx]
        a_ab = [jnp.where(strict, p[:CHUNK, :PAIR], 0.0) for p in p_all]
        a_ak = [jnp.where(strict, p[:CHUNK, PAIR:], 0.0) for p in p_all]
        a_rb = [jnp.where(incl, p[CHUNK:, :PAIR], 0.0) for p in p_all]
        a_rk = [jnp.where(incl, p[CHUNK:, PAIR:], 0.0) for p in p_all]
        av = [_dot(cat([a_ak[i], a_rk[i]], axis=0), bd(vv[i])) for i in idx]
        a_d = [jnp.where(same16, a, 0.0) for a in a_ab]
        t_m = [eye + a for a in a_d]
        p_m = [pdot(a, a) for a in a_d]
        for _ in range(2):
            res = [pdot(cat([p_m[i], t_m[i]], axis=0), p_m[i]) for i in idx]
            t_m = [t_m[i] + res[i][CHUNK:] for i in idx]
            p_m = [r[:CHUNK] for r in res]
        t_m = [t_m[i] + pdot(t_m[i], p_m[i]) for i in idx]
        for off in (jnp.logical_and(same32, jnp.logical_not(same16)), jnp.logical_not(same32)):
            x_m = [pdot(t_m[i], jnp.where(off, a_ab[i], 0.0)) for i in idx]
            t_m = [t_m[i] + pdot(x_m[i], t_m[i]) for i in idx]
        tx = [_dot(t_m[i], cat([bd(at[i]), bd(av[i][:CHUNK].astype(BF16))], axis=1)) for i in idx]
        at2 = [t[:, :PAIR] for t in tx]
        w_u = [t[:, PAIR:] for t in tx]
        e1 = [_dot(a_rb[i], cat([bd(at2[i].astype(BF16)), bd(w_u[i].astype(BF16))], axis=1)) for i in idx]
        zeros = jnp.zeros((CHUNK, PAIR), F32)
        lhs_t = [cat([cat([at2[i], zeros], axis=0).T, cat([w_u[i], vv[i].astype(F32)], axis=0).T], axis=0)
                 for i in idx]
        e2 = [_dot(lhs_t[i], cat([bh_ref[b, 0, sl, :], kh_ref[b, 0, sl, :]], axis=0))
              for i, (b, sl) in enumerate(where)]
        for i, (b, sl) in enumerate(where):
            ci = it * P1_CHUNKS + i // n_batch
            r2_ref[b, sl, :] = (rt[i].astype(F32) + e1[i][:, :PAIR]).astype(BF16)
            y0_ref[b, sl, :] = av[i][CHUNK:] + e1[i][:, PAIR:]
            mlr_ref[b, ci] = jnp.where(same_head, e2[i][:PAIR], 0.0).astype(BF16)
            nc_ref[b, ci] = jnp.where(same_head, e2[i][PAIR:], 0.0)
        return carry

    lax.fori_loop(0, rows // (CHUNK * P1_CHUNKS), phase1, 0)

    def phase2(ci, carry):
        sl = pl.ds(pl.multiple_of(ci * CHUNK, CHUNK), CHUNK)
        bs = range(n_batch)
        s_old = [s_ref[b] for b in bs]
        s_bf = [s.astype(BF16) for s in s_old]
        r2 = [r2_ref[b, sl, :] for b in bs]
        y0 = [y0_ref[b, sl, :] for b in bs]
        mlr = [mlr_ref[b, ci] for b in bs]
        n_c = [nc_ref[b, ci] for b in bs]
        decay = [gl_ref[b, 0, ci] for b in bs]
        s_new = [s_old[b] * decay[b] + _dot(s_bf[b], mlr[b]) + n_c[b] for b in bs]
        y = [_dot_nt(r2[b], s_bf[b]) + y0[b] for b in bs]
        for b in bs:
            s_ref[b] = s_new[b]
            y_ref[b, 0, sl, :] = y[b]
        return carry

    lax.fori_loop(0, rows // CHUNK, phase2, 0)

    y = y_ref[:, 0].reshape(n_batch * rows, PAIR)
    ones = ones_ref[...]

    def head_mean(z):
        hi, lo = _split2(z)
        return (_dot(hi, ones) + _dot(lo, ones)) * (1.0 / HEAD)

    d = y - head_mean(y)
    yn = d * lax.rsqrt(head_mean(d * d) + GN_EPS) * gnw_ref[0] + gnb_ref[0]
    yn = yn.reshape(n_batch, rows, PAIR) + bonus_ref[:, 0]
    y_ref[:, 0] = yn


def _mix_ffn_kernel(x_ref, y_ref, g_ref, wo_ref, gains_ref, wg_ref, wu_ref, wd_ref, o_ref):
    y = jnp.concatenate([y_ref[0, p] for p in range(N_PAIRS)], axis=-1)
    m = _dot(y * g_ref[0], wo_ref[...])
    x1 = x_ref[0] + _rmsnorm(m, gains_ref[0:1])
    f = _ffn(_rmsnorm(x1, gains_ref[1:2]), wg_ref, wu_ref, wd_ref)
    o_ref[0] = x1 + _rmsnorm(f, gains_ref[2:3])


def _sgu_ffn_kernel(x_ref, gains_ref, win_ref, bin_ref, lnw_ref, lnb_ref, ws_ref, bs_ref,
                    wout_ref, wg_ref, wu_ref, wd_ref, o_ref):
    x = x_ref[0]
    rows = x.shape[0]
    hb = _rmsnorm(x, gains_ref[0:1]).astype(BF16)
    u = _gelu(_dot(hb, win_ref[:, :D_MODEL]) + bin_ref[:, :D_MODEL])
    v = _gelu(_dot(hb, win_ref[:, D_MODEL:]) + bin_ref[:, D_MODEL:])
    mu = jnp.mean(v, axis=-1, keepdims=True)
    dv = v - mu
    var = jnp.mean(dv * dv, axis=-1, keepdims=True)
    vn = (dv * lax.rsqrt(var + LN_EPS) * lnw_ref[...] + lnb_ref[...]).astype(BF16)

    ri = lax.broadcasted_iota(jnp.int32, (SGU_BLOCK, SGU_BLOCK), 0)
    ci = lax.broadcasted_iota(jnp.int32, (SGU_BLOCK, SGU_BLOCK), 1)
    causal = (ci // CHUNK) <= (ri // CHUNK)
    cols = []
    for gi in range(SGU_GROUPS):
        ws_m = jnp.where(causal, ws_ref[gi], 0.0).astype(BF16)
        lanes = slice(gi * SGU_BLOCK, (gi + 1) * SGU_BLOCK)
        blocks = [_dot(ws_m, vn[n * SGU_BLOCK:(n + 1) * SGU_BLOCK, lanes]) + bs_ref[:, lanes]
                  for n in range(rows // SGU_BLOCK)]
        cols.append(jnp.concatenate(blocks, axis=0))
    s = jnp.concatenate(cols, axis=-1)
    m = _dot(u * s, wout_ref[...])
    x1 = x + _rmsnorm(m, gains_ref[1:2])
    f = _ffn(_rmsnorm(x1, gains_ref[2:3]), wg_ref, wu_ref, wd_ref)
    o_ref[0] = x1 + _rmsnorm(f, gains_ref[3:4])


def _resident(shape):
    zeros = (0,) * len(shape)
    return pl.BlockSpec(shape, lambda *_: zeros, pipeline_mode=pl.Buffered(1))


def kernel(x, norm_gains, rwkv_mix, rwkv_w_rkv, rwkv_w0, rwkv_w1, rwkv_w2, rwkv_a0, rwkv_a1, rwkv_a2, rwkv_g1, rwkv_g2, rwkv_k_k, rwkv_k_a, rwkv_r_k, rwkv_gn_w, rwkv_gn_b, rwkv_w_o, sgu_w_in, sgu_b_in, sgu_ln_w, sgu_ln_b, sgu_ws, sgu_bs, sgu_w_out, ffn_w_gate, ffn_w_up, ffn_w_down):
    n_b, n_t, d = x.shape
    assert d == D_MODEL and n_t % SCAN_ROWS == 0 and n_t % FFN_ROWS == 0 and n_t % PRE_ROWS == 0
    row2 = lambda p: p.reshape(1, -1)

    ch = jnp.arange(D_MODEL) // HEAD
    e_mat = (ch[:, None] == jnp.arange(PAIR)[None, :]).astype(BF16)
    et_mat = e_mat.T
    tt = jnp.arange(PRE_ROWS)
    tri = jnp.logical_and(tt[:, None] // CHUNK == tt[None, :] // CHUNK,
                          tt[None, :] <= tt[:, None]).astype(BF16)
    sel = (jnp.arange(8)[:, None] == tt[None, :] // CHUNK).astype(BF16)
    pl_idx = jnp.arange(PAIR) // HEAD
    ones_blk = (pl_idx[:, None] == pl_idx[None, :]).astype(BF16)

    n_pre = n_t // PRE_ROWS
    slab = jax.ShapeDtypeStruct((n_b, N_PAIRS, n_t, PAIR), BF16)
    slab_spec = pl.BlockSpec((1, N_PAIRS, PRE_ROWS, PAIR), lambda b, i: (b, 0, i, 0))
    pre_out = pl.pallas_call(
        _rwkv_pre_kernel,
        grid=(n_b, n_pre),
        in_specs=[
            pl.BlockSpec((1, PRE_ROWS, D_MODEL), lambda b, i: (b, i, 0)),
            pl.BlockSpec((1, 8, D_MODEL), lambda b, i: (b, jnp.maximum(i * (PRE_ROWS // 8) - 1, 0), 0)),
            _resident((1, D_MODEL)),
            _resident((6, D_MODEL)),
            _resident((3, D_MODEL, D_MODEL)),
            _resident((1, D_MODEL)), _resident(rwkv_w1.shape[1:]), _resident(rwkv_w2.shape[1:]),
            _resident((1, D_MODEL)), _resident(rwkv_a1.shape[1:]), _resident(rwkv_a2.shape[1:]),
            _resident(rwkv_g1.shape[1:]), _resident(rwkv_g2.shape[1:]),
            _resident((1, D_MODEL)), _resident((1, D_MODEL)), _resident((1, D_MODEL)),
            _resident(e_mat.shape), _resident(et_mat.shape), _resident(tri.shape), _resident(sel.shape),
        ],
        out_specs=[slab_spec] * 7 + [
            slab_spec,
            pl.BlockSpec((1, PRE_ROWS, D_MODEL), lambda b, i: (b, i, 0)),
            pl.BlockSpec((1, N_PAIRS, PRE_ROWS // CHUNK, 1, PAIR), lambda b, i: (b, 0, i, 0, 0)),
        ],
        out_shape=[slab] * 7 + [
            jax.ShapeDtypeStruct((n_b, N_PAIRS, n_t, PAIR), F32),
            jax.ShapeDtypeStruct((n_b, n_t, D_MODEL), F32),
            jax.ShapeDtypeStruct((n_b, N_PAIRS, n_t // CHUNK, 1, PAIR), F32),
        ],
        compiler_params=pltpu.CompilerParams(
            dimension_semantics=("arbitrary", "arbitrary"), vmem_limit_bytes=VMEM_LIMIT),
        name="rwkv_pre",
    )(x, x, row2(norm_gains[0, 0]), rwkv_mix[0], rwkv_w_rkv[0].astype(BF16),
      row2(rwkv_w0[0]), rwkv_w1[0].astype(BF16), rwkv_w2[0].astype(BF16),
      row2(rwkv_a0[0]), rwkv_a1[0].astype(BF16), rwkv_a2[0].astype(BF16),
      rwkv_g1[0].astype(BF16), rwkv_g2[0].astype(BF16),
      row2(rwkv_k_k[0]), row2(rwkv_k_a[0]), row2(rwkv_r_k[0]),
      e_mat, et_mat, tri, sel)
    rt, at, kt, bt, kh, bh, vv, bonus, gate, gl = pre_out

    scan_spec = pl.BlockSpec((n_b, 1, SCAN_ROWS, PAIR), lambda p, t: (0, p, t, 0))
    y = pl.pallas_call(
        _rwkv_scan_kernel,
        grid=(N_PAIRS, n_t // SCAN_ROWS),
        in_specs=[scan_spec] * 8 + [
            pl.BlockSpec((n_b, 1, SCAN_ROWS // CHUNK, 1, PAIR), lambda p, t: (0, p, t, 0, 0)),
            pl.BlockSpec((1, 1, PAIR), lambda p, t: (p, 0, 0)),
            pl.BlockSpec((1, 1, PAIR), lambda p, t: (p, 0, 0)),
            _resident(ones_blk.shape),
        ],
        out_specs=scan_spec,
        out_shape=jax.ShapeDtypeStruct((n_b, N_PAIRS, n_t, PAIR), F32),
        scratch_shapes=[pltpu.VMEM((n_b, PAIR, PAIR), F32),
                        pltpu.VMEM((n_b, SCAN_ROWS, PAIR), BF16),
                        pltpu.VMEM((n_b, SCAN_ROWS, PAIR), F32),
                        pltpu.VMEM((n_b, SCAN_ROWS // CHUNK, PAIR, PAIR), BF16),
                        pltpu.VMEM((n_b, SCAN_ROWS // CHUNK, PAIR, PAIR), F32)],
        compiler_params=pltpu.CompilerParams(
            dimension_semantics=("arbitrary", "arbitrary"), vmem_limit_bytes=VMEM_LIMIT),
        name="rwkv_scan",
    )(rt, at, kt, bt, kh, bh, vv, bonus, gl,
      rwkv_gn_w[0].reshape(N_PAIRS, 1, PAIR), rwkv_gn_b[0].reshape(N_PAIRS, 1, PAIR), ones_blk)

    n_ffn = n_t // FFN_ROWS
    row_spec = pl.BlockSpec((1, FFN_ROWS, D_MODEL), lambda b, i: (b, i, 0))
    ffn_specs = [_resident((D_MODEL, D_FF)), _resident((D_MODEL, D_FF)), _resident((D_FF, D_MODEL))]
    x = pl.pallas_call(
        _mix_ffn_kernel,
        grid=(n_b, n_ffn),
        in_specs=[
            row_spec,
            pl.BlockSpec((1, N_PAIRS, FFN_ROWS, PAIR), lambda b, i: (b, 0, i, 0)),
            row_spec,
            _resident((D_MODEL, D_MODEL)),
            _resident((3, D_MODEL)),
        ] + ffn_specs,
        out_specs=row_spec,
        out_shape=jax.ShapeDtypeStruct((n_b, n_t, D_MODEL), F32),
        compiler_params=pltpu.CompilerParams(
            dimension_semantics=("arbitrary", "arbitrary"), vmem_limit_bytes=VMEM_LIMIT),
        name="mix_ffn",
    )(x, y, gate, rwkv_w_o[0].astype(BF16), norm_gains[0, 1:4],
      ffn_w_gate[0].astype(BF16), ffn_w_up[0].astype(BF16), ffn_w_down[0].astype(BF16))

    bs_full = jnp.repeat(sgu_bs[0].T, SGU_BLOCK, axis=1)
    x = pl.pallas_call(
        _sgu_ffn_kernel,
        grid=(n_b, n_ffn),
        in_specs=[
            row_spec,
            _resident((4, D_MODEL)),
            _resident((D_MODEL, 2 * D_MODEL)),
            _resident((1, 2 * D_MODEL)),
            _resident((1, D_MODEL)), _resident((1, D_MODEL)),
            _resident((SGU_GROUPS, SGU_BLOCK, SGU_BLOCK)),
            _resident((SGU_BLOCK, D_MODEL)),
            _resident((D_MODEL, D_MODEL)),
        ] + ffn_specs,
        out_specs=row_spec,
        out_shape=jax.ShapeDtypeStruct((n_b, n_t, D_MODEL), F32),
        compiler_params=pltpu.CompilerParams(
            dimension_semantics=("arbitrary", "arbitrary"), vmem_limit_bytes=VMEM_LIMIT),
        name="sgu_ffn",
    )(x, norm_gains[1], sgu_w_in[0].astype(BF16), row2(sgu_b_in[0]),
      row2(sgu_ln_w[0]), row2(sgu_ln_b[0]), sgu_ws[0], bs_full, sgu_w_out[0].astype(BF16),
      ffn_w_gate[1].astype(BF16), ffn_w_up[1].astype(BF16), ffn_w_down[1].astype(BF16))
    return x
```

```python
import functools
import math

import jax
import jax.numpy as jnp
from jax import lax
from jax.experimental import pallas as pl
from jax.experimental.pallas import tpu as pltpu

F32 = jnp.float32
BF16 = jnp.bfloat16

D_MODEL = 1024
HEAD = 64
PAIR = 2 * HEAD
N_PAIRS = D_MODEL // PAIR
CHUNK = 64
SGU_BLOCK = 128
SGU_GROUPS = 8
D_FF = 2816
RMS_EPS = 1e-6
GN_EPS = 64e-5
LN_EPS = 1e-5

PRE_ROWS = 256
SCAN_ROWS = 512
SCAN_PAIRS = 2
P1_CHUNKS = 2
FFN_ROWS = 512
FFN_SLABS = ((0, 1536), (1536, D_FF))
VMEM_LIMIT = 56 * 1024 * 1024


def _dot(a, b):
    return jnp.dot(a.astype(BF16), b.astype(BF16), preferred_element_type=F32)


def _dot_nt(a, b):
    return lax.dot_general(a.astype(BF16), b.astype(BF16), (((1,), (1,)), ((), ())),
                           preferred_element_type=F32)


def _split2(x):
    hi = x.astype(BF16)
    lo = (x - hi.astype(F32)).astype(BF16)
    return hi, lo


def _rmsnorm(x, g):
    return x * lax.rsqrt(jnp.mean(x * x, axis=-1, keepdims=True) + RMS_EPS) * g


def _softplus(z):
    return jnp.maximum(z, 0.0) + jnp.log1p(jnp.exp(-jnp.abs(z)))


def _gelu(x):
    return x * (lax.erf(x * (1.0 / math.sqrt(2.0))) + 1.0) * 0.5


def _ffn(h, wg_ref, wu_ref, wd_ref):
    hb = h.astype(BF16)
    acc = None
    for lo, hi in FFN_SLABS:
        sl = slice(lo, hi)
        gate = _dot(hb, wg_ref[:, sl])
        up = _dot(hb, wu_ref[:, sl])
        part = _dot(gate * jax.nn.sigmoid(gate) * up, wd_ref[sl, :])
        acc = part if acc is None else acc + part
    return acc


def _rwkv_pre_kernel(x_ref, xp_ref, gain_ref, mix_ref, wrkv_ref, w0_ref, w1_ref, w2_ref,
                     a0_ref, a1_ref, a2_ref, g1_ref, g2_ref, kk_ref, ka_ref, rk_ref,
                     e_ref, et_ref, tri_ref, sel_ref,
                     rt_ref, at_ref, kt_ref, bt_ref, kh_ref, bh_ref, v_ref, bonus_ref,
                     g_ref, gl_ref):
    i = pl.program_id(1)
    gain = gain_ref[...]
    h = _rmsnorm(x_ref[0], gain)
    hp = _rmsnorm(xp_ref[0], gain)[7:8]
    hp = jnp.where(i > 0, hp, 0.0)
    row = lax.broadcasted_iota(jnp.int32, h.shape, 0)
    h_prev = jnp.where(row == 0, hp, pltpu.roll(h, 1, 0))
    xx = h_prev - h
    mix = mix_ref[...]

    def shifted(c):
        return (h + xx * mix[c:c + 1]).astype(BF16)

    r = _dot(shifted(0), wrkv_ref[0])
    k = _dot(shifted(1), wrkv_ref[1])
    v = _dot(shifted(2), wrkv_ref[2])
    w_raw = w0_ref[...] + _dot(jnp.tanh(_dot(shifted(3), w1_ref[...])), w2_ref[...])
    lw = -jnp.exp(-_softplus(-w_raw) - 0.5)
    a = jax.nn.sigmoid(a0_ref[...] + _dot(_dot(shifted(4), a1_ref[...]), a2_ref[...]))
    g_ref[0] = _dot(jax.nn.sigmoid(_dot(shifted(5), g1_ref[...])), g2_ref[...])

    def head_sum(q, parts):
        s = _dot(q, e_ref[...])
        return sum(_dot(p, et_ref[...]) for p in (_split2(s) if parts == 2 else (s,)))

    kkr = k * kk_ref[...]
    ss = head_sum(kkr * kkr, 2)
    kk = kkr / jnp.maximum(jnp.sqrt(ss), 1e-12)
    k2 = k * (1.0 + (a - 1.0) * ka_ref[...])
    bonus = head_sum(r * k2 * rk_ref[...], 1) * v
    avec = -kk
    bvec = kk * a

    parts = _split2(lw)
    c = sum(_dot(tri_ref[...], p) for p in parts)
    tot = sum(_dot(sel_ref[...], p) for p in parts)
    n_chunks = h.shape[0] // CHUNK
    c_last = jnp.concatenate(
        [jnp.broadcast_to(tot[j:j + 1], (CHUNK, D_MODEL)) for j in range(n_chunks)], axis=0)
    e_neg = jnp.exp(-c)
    e_rest = jnp.exp(c_last - c)
    outs = (
        (rt_ref, r * jnp.exp(c)),
        (at_ref, avec * jnp.exp(c - lw)),
        (kt_ref, k2 * e_neg),
        (bt_ref, bvec * e_neg),
        (kh_ref, k2 * e_rest),
        (bh_ref, bvec * e_rest),
        (v_ref, v),
        (bonus_ref, bonus),
    )
    gl = jnp.exp(tot[:n_chunks])
    for p in range(N_PAIRS):
        sl = slice(p * PAIR, (p + 1) * PAIR)
        for ref, val in outs:
            ref[0, p] = val[:, sl].astype(ref.dtype)
        for j in range(n_chunks):
            gl_ref[0, p, j] = gl[j:j + 1, sl]


def _rwkv_scan_kernel(rt_ref, at_ref, kt_ref, bt_ref, kh_ref, bh_ref, v_ref, bonus_ref, gl_ref,
                      gnw_ref, gnb_ref, ones_ref, y_ref, s_ref, r2_ref, y0_ref, mlr_ref, nc_ref):
    n_batch, n_pp, rows = rt_ref.shape[0], rt_ref.shape[1], rt_ref.shape[2]
    seqs = [(b, q) for b in range(n_batch) for q in range(n_pp)]

    @pl.when(pl.program_id(1) == 0)
    def _():
        s_ref[...] = jnp.zeros_like(s_ref)

    lane = lax.broadcasted_iota(jnp.int32, (CHUNK, PAIR), 1)
    trow = lax.broadcasted_iota(jnp.int32, (CHUNK, PAIR), 0)
    tcol = lane & (HEAD - 1)
    first_head = lane < HEAD
    strict = tcol < trow
    incl = tcol <= trow
    same16 = (tcol >> 4) == (trow >> 4)
    same32 = (tcol >> 5) == (trow >> 5)
    eye = jnp.where(tcol == trow, 1.0, 0.0).astype(F32)
    r2 = lax.broadcasted_iota(jnp.int32, (PAIR, PAIR), 0)
    c2 = lax.broadcasted_iota(jnp.int32, (PAIR, PAIR), 1)
    same_head = (r2 >> 6) == (c2 >> 6)

    def bd(y):
        z = jnp.zeros_like(y)
        return jnp.concatenate([jnp.where(first_head, y, z), jnp.where(first_head, z, y)], axis=0)

    def pdot(x, y):
        return _dot(x, bd(y.astype(BF16)))

    cat = jnp.concatenate

    def phase1(it, carry):
        where = [(b, q, n, it * P1_CHUNKS + cj)
                 for cj in range(P1_CHUNKS) for n, (b, q) in enumerate(seqs)]
        sls = [pl.ds(pl.multiple_of(ci * CHUNK, CHUNK), CHUNK) for _, _, _, ci in where]
        idx = range(len(where))
        rt = [rt_ref[where[i][0], where[i][1], sls[i], :] for i in idx]
        at = [at_ref[where[i][0], where[i][1], sls[i], :] for i in idx]
        kt = [kt_ref[where[i][0], where[i][1], sls[i], :] for i in idx]
        bt = [bt_ref[where[i][0], where[i][1], sls[i], :] for i in idx]
        vv = [v_ref[where[i][0], where[i][1], sls[i], :] for i in idx]
        bk = [cat([bh_ref[where[i][0], where[i][1], sls[i], :],
                   kh_ref[where[i][0], where[i][1], sls[i], :]], axis=0) for i in idx]
        p_all = [_dot_nt(cat([at[i], rt[i]], axis=0), cat([bd(bt[i]), bd(kt[i])], axis=0)) for i in idx]
        a_ab = [jnp.where(strict, p[:CHUNK, :PAIR], 0.0) for p in p_all]
        a_ak = [jnp.where(strict, p[:CHUNK, PAIR:], 0.0) for p in p_all]
        a_rb = [jnp.where(incl, p[CHUNK:, :PAIR], 0.0) for p in p_all]
        a_rk = [jnp.where(incl, p[CHUNK:, PAIR:], 0.0) for p in p_all]
        av = [_dot(cat([a_ak[i], a_rk[i]], axis=0), bd(vv[i])) for i in idx]
        a_d = [jnp.where(same16, a, 0.0) for a in a_ab]
        t_m = [eye + a for a in a_d]
        p_m = [pdot(a, a) for a in a_d]
        for _ in range(2):
            res = [pdot(cat([p_m[i], t_m[i]], axis=0), p_m[i]) for i in idx]
            t_m = [t_m[i] + res[i][CHUNK:] for i in idx]
            p_m = [r[:CHUNK] for r in res]
        t_m = [t_m[i] + pdot(t_m[i], p_m[i]) for i in idx]
        for off in (jnp.logical_and(same32, jnp.logical_not(same16)), jnp.logical_not(same32)):
            x_m = [pdot(t_m[i], jnp.where(off, a_ab[i], 0.0)) for i in idx]
            t_m = [t_m[i] + pdot(x_m[i], t_m[i]) for i in idx]
        tx = [_dot(t_m[i], cat([bd(at[i]), bd(av[i][:CHUNK].astype(BF16))], axis=1)) for i in idx]
        at2 = [t[:, :PAIR] for t in tx]
        w_u = [t[:, PAIR:] for t in tx]
        e1 = [_dot(a_rb[i], cat([bd(at2[i].astype(BF16)), bd(w_u[i].astype(BF16))], axis=1)) for i in idx]
        zeros = jnp.zeros((CHUNK, PAIR), F32)
        lhs_t = [cat([cat([at2[i], zeros], axis=0).T, cat([w_u[i], vv[i].astype(F32)], axis=0).T], axis=0)
                 for i in idx]
        e2 = [_dot(lhs_t[i], bk[i]) for i in idx]
        for i in idx:
            _, _, n, ci = where[i]
            r2_ref[n, sls[i], :] = (rt[i].astype(F32) + e1[i][:, :PAIR]).astype(BF16)
            y0_ref[n, sls[i], :] = av[i][CHUNK:] + e1[i][:, PAIR:]
            mlr_ref[n, ci] = jnp.where(same_head, e2[i][:PAIR], 0.0).astype(BF16)
            nc_ref[n, ci] = jnp.where(same_head, e2[i][PAIR:], 0.0)
        return carry

    lax.fori_loop(0, rows // (CHUNK * P1_CHUNKS), phase1, 0)

    def phase2(ci, carry):
        sl = pl.ds(pl.multiple_of(ci * CHUNK, CHUNK), CHUNK)
        ns = range(len(seqs))
        s_old = [s_ref[n] for n in ns]
        s_bf = [s.astype(BF16) for s in s_old]
        r2 = [r2_ref[n, sl, :] for n in ns]
        y0 = [y0_ref[n, sl, :] for n in ns]
        mlr = [mlr_ref[n, ci] for n in ns]
        n_c = [nc_ref[n, ci] for n in ns]
        decay = [gl_ref[b, q, ci] for b, q in seqs]
        s_new = [s_old[n] * decay[n] + _dot(s_bf[n], mlr[n]) + n_c[n] for n in ns]
        y = [_dot_nt(r2[n], s_bf[n]) + y0[n] for n in ns]
        for n, (b, q) in enumerate(seqs):
            s_ref[n] = s_new[n]
            y_ref[b, q, sl, :] = y[n]
        return carry

    lax.fori_loop(0, rows // CHUNK, phase2, 0)

    ones = ones_ref[...]

    def head_mean(z):
        return _dot(z, ones) * (1.0 / HEAD)

    for q in range(n_pp):
        y = y_ref[:, q].reshape(n_batch * rows, PAIR)
        d = y - head_mean(y)
        yn = d * lax.rsqrt(head_mean(d * d) + GN_EPS) * gnw_ref[q] + gnb_ref[q]
        y_ref[:, q] = yn.reshape(n_batch, rows, PAIR) + bonus_ref[:, q]


def _mix_ffn_kernel(x_ref, y_ref, g_ref, wo_ref, gains_ref, wg_ref, wu_ref, wd_ref, o_ref):
    y = jnp.concatenate([y_ref[0, p] for p in range(N_PAIRS)], axis=-1)
    m = _dot(y * g_ref[0], wo_ref[...])
    x1 = x_ref[0] + _rmsnorm(m, gains_ref[0:1])
    f = _ffn(_rmsnorm(x1, gains_ref[1:2]), wg_ref, wu_ref, wd_ref)
    o_ref[0] = x1 + _rmsnorm(f, gains_ref[2:3])


def _sgu_ffn_kernel(x_ref, gains_ref, win_ref, bin_ref, lnw_ref, lnb_ref, ws_ref, bs_ref,
                    wout_ref, wg_ref, wu_ref, wd_ref, o_ref):
    x = x_ref[0]
    rows = x.shape[0]
    hb = _rmsnorm(x, gains_ref[0:1]).astype(BF16)
    u = _gelu(_dot(hb, win_ref[:, :D_MODEL]) + bin_ref[:, :D_MODEL])
    v = _gelu(_dot(hb, win_ref[:, D_MODEL:]) + bin_ref[:, D_MODEL:])
    mu = jnp.mean(v, axis=-1, keepdims=True)
    dv = v - mu
    var = jnp.mean(dv * dv, axis=-1, keepdims=True)
    vn = (dv * lax.rsqrt(var + LN_EPS) * lnw_ref[...] + lnb_ref[...]).astype(BF16)

    ri = lax.broadcasted_iota(jnp.int32, (SGU_BLOCK, SGU_BLOCK), 0)
    ci = lax.broadcasted_iota(jnp.int32, (SGU_BLOCK, SGU_BLOCK), 1)
    causal = (ci // CHUNK) <= (ri // CHUNK)
    cols = []
    for gi in range(SGU_GROUPS):
        ws_m = jnp.where(causal, ws_ref[gi], 0.0).astype(BF16)
        lanes = slice(gi * SGU_BLOCK, (gi + 1) * SGU_BLOCK)
        blocks = [_dot(ws_m, vn[n * SGU_BLOCK:(n + 1) * SGU_BLOCK, lanes]) + bs_ref[:, lanes]
                  for n in range(rows // SGU_BLOCK)]
        cols.append(jnp.concatenate(blocks, axis=0))
    s = jnp.concatenate(cols, axis=-1)
    m = _dot(u * s, wout_ref[...])
    x1 = x + _rmsnorm(m, gains_ref[1:2])
    f = _ffn(_rmsnorm(x1, gains_ref[2:3]), wg_ref, wu_ref, wd_ref)
    o_ref[0] = x1 + _rmsnorm(f, gains_ref[3:4])


def _resident(shape):
    zeros = (0,) * len(shape)
    return pl.BlockSpec(shape, lambda *_: zeros, pipeline_mode=pl.Buffered(1))


def kernel(x, norm_gains, rwkv_mix, rwkv_w_rkv, rwkv_w0, rwkv_w1, rwkv_w2, rwkv_a0, rwkv_a1, rwkv_a2, rwkv_g1, rwkv_g2, rwkv_k_k, rwkv_k_a, rwkv_r_k, rwkv_gn_w, rwkv_gn_b, rwkv_w_o, sgu_w_in, sgu_b_in, sgu_ln_w, sgu_ln_b, sgu_ws, sgu_bs, sgu_w_out, ffn_w_gate, ffn_w_up, ffn_w_down):
    n_b, n_t, d = x.shape
    assert d == D_MODEL and n_t % SCAN_ROWS == 0 and n_t % FFN_ROWS == 0 and n_t % PRE_ROWS == 0
    row2 = lambda p: p.reshape(1, -1)

    ch = jnp.arange(D_MODEL) // HEAD
    e_mat = (ch[:, None] == jnp.arange(PAIR)[None, :]).astype(BF16)
    et_mat = e_mat.T
    tt = jnp.arange(PRE_ROWS)
    tri = jnp.logical_and(tt[:, None] // CHUNK == tt[None, :] // CHUNK,
                          tt[None, :] <= tt[:, None]).astype(BF16)
    sel = (jnp.arange(8)[:, None] == tt[None, :] // CHUNK).astype(BF16)
    pl_idx = jnp.arange(PAIR) // HEAD
    ones_blk = (pl_idx[:, None] == pl_idx[None, :]).astype(BF16)

    n_pre = n_t // PRE_ROWS
    slab = jax.ShapeDtypeStruct((n_b, N_PAIRS, n_t, PAIR), BF16)
    slab_spec = pl.BlockSpec((1, N_PAIRS, PRE_ROWS, PAIR), lambda b, i: (b, 0, i, 0))
    pre_out = pl.pallas_call(
        _rwkv_pre_kernel,
        grid=(n_b, n_pre),
        in_specs=[
            pl.BlockSpec((1, PRE_ROWS, D_MODEL), lambda b, i: (b, i, 0)),
            pl.BlockSpec((1, 8, D_MODEL), lambda b, i: (b, jnp.maximum(i * (PRE_ROWS // 8) - 1, 0), 0)),
            _resident((1, D_MODEL)),
            _resident((6, D_MODEL)),
            _resident((3, D_MODEL, D_MODEL)),
            _resident((1, D_MODEL)), _resident(rwkv_w1.shape[1:]), _resident(rwkv_w2.shape[1:]),
            _resident((1, D_MODEL)), _resident(rwkv_a1.shape[1:]), _resident(rwkv_a2.shape[1:]),
            _resident(rwkv_g1.shape[1:]), _resident(rwkv_g2.shape[1:]),
            _resident((1, D_MODEL)), _resident((1, D_MODEL)), _resident((1, D_MODEL)),
            _resident(e_mat.shape), _resident(et_mat.shape), _resident(tri.shape), _resident(sel.shape),
        ],
        out_specs=[slab_spec] * 7 + [
            slab_spec,
            pl.BlockSpec((1, PRE_ROWS, D_MODEL), lambda b, i: (b, i, 0)),
            pl.BlockSpec((1, N_PAIRS, PRE_ROWS // CHUNK, 1, PAIR), lambda b, i: (b, 0, i, 0, 0)),
        ],
        out_shape=[slab] * 7 + [
            jax.ShapeDtypeStruct((n_b, N_PAIRS, n_t, PAIR), F32),
            jax.ShapeDtypeStruct((n_b, n_t, D_MODEL), F32),
            jax.ShapeDtypeStruct((n_b, N_PAIRS, n_t // CHUNK, 1, PAIR), F32),
        ],
        compiler_params=pltpu.CompilerParams(
            dimension_semantics=("arbitrary", "arbitrary"), vmem_limit_bytes=VMEM_LIMIT),
        name="rwkv_pre",
    )(x, x, row2(norm_gains[0, 0]), rwkv_mix[0], rwkv_w_rkv[0].astype(BF16),
      row2(rwkv_w0[0]), rwkv_w1[0].astype(BF16), rwkv_w2[0].astype(BF16),
      row2(rwkv_a0[0]), rwkv_a1[0].astype(BF16), rwkv_a2[0].astype(BF16),
      rwkv_g1[0].astype(BF16), rwkv_g2[0].astype(BF16),
      row2(rwkv_k_k[0]), row2(rwkv_k_a[0]), row2(rwkv_r_k[0]),
      e_mat, et_mat, tri, sel)
    rt, at, kt, bt, kh, bh, vv, bonus, gate, gl = pre_out

    scan_spec = pl.BlockSpec((n_b, SCAN_PAIRS, SCAN_ROWS, PAIR), lambda p, t: (0, p, t, 0))
    n_seq = n_b * SCAN_PAIRS
    y = pl.pallas_call(
        _rwkv_scan_kernel,
        grid=(N_PAIRS // SCAN_PAIRS, n_t // SCAN_ROWS),
        in_specs=[scan_spec] * 8 + [
            pl.BlockSpec((n_b, SCAN_PAIRS, SCAN_ROWS // CHUNK, 1, PAIR), lambda p, t: (0, p, t, 0, 0)),
            pl.BlockSpec((SCAN_PAIRS, 1, PAIR), lambda p, t: (p, 0, 0)),
            pl.BlockSpec((SCAN_PAIRS, 1, PAIR), lambda p, t: (p, 0, 0)),
            _resident(ones_blk.shape),
        ],
        out_specs=scan_spec,
        out_shape=jax.ShapeDtypeStruct((n_b, N_PAIRS, n_t, PAIR), F32),
        scratch_shapes=[pltpu.VMEM((n_seq, PAIR, PAIR), F32),
                        pltpu.VMEM((n_seq, SCAN_ROWS, PAIR), BF16),
                        pltpu.VMEM((n_seq, SCAN_ROWS, PAIR), F32),
                        pltpu.VMEM((n_seq, SCAN_ROWS // CHUNK, PAIR, PAIR), BF16),
                        pltpu.VMEM((n_seq, SCAN_ROWS // CHUNK, PAIR, PAIR), F32)],
        compiler_params=pltpu.CompilerParams(
            dimension_semantics=("arbitrary", "arbitrary"), vmem_limit_bytes=VMEM_LIMIT),
        name="rwkv_scan",
    )(rt, at, kt, bt, kh, bh, vv, bonus, gl,
      rwkv_gn_w[0].reshape(N_PAIRS, 1, PAIR), rwkv_gn_b[0].reshape(N_PAIRS, 1, PAIR), ones_blk)

    n_ffn = n_t // FFN_ROWS
    row_spec = pl.BlockSpec((1, FFN_ROWS, D_MODEL), lambda b, i: (b, i, 0))
    ffn_specs = [_resident((D_MODEL, D_FF)), _resident((D_MODEL, D_FF)), _resident((D_FF, D_MODEL))]
    x = pl.pallas_call(
        _mix_ffn_kernel,
        grid=(n_b, n_ffn),
        in_specs=[
            row_spec,
            pl.BlockSpec((1, N_PAIRS, FFN_ROWS, PAIR), lambda b, i: (b, 0, i, 0)),
            row_spec,
            _resident((D_MODEL, D_MODEL)),
            _resident((3, D_MODEL)),
        ] + ffn_specs,
        out_specs=row_spec,
        out_shape=jax.ShapeDtypeStruct((n_b, n_t, D_MODEL), F32),
        compiler_params=pltpu.CompilerParams(
            dimension_semantics=("arbitrary", "arbitrary"), vmem_limit_bytes=VMEM_LIMIT),
        name="mix_ffn",
    )(x, y, gate, rwkv_w_o[0].astype(BF16), norm_gains[0, 1:4],
      ffn_w_gate[0].astype(BF16), ffn_w_up[0].astype(BF16), ffn_w_down[0].astype(BF16))

    bs_full = jnp.repeat(sgu_bs[0].T, SGU_BLOCK, axis=1)
    x = pl.pallas_call(
        _sgu_ffn_kernel,
        grid=(n_b, n_ffn),
        in_specs=[
            row_spec,
            _resident((4, D_MODEL)),
            _resident((D_MODEL, 2 * D_MODEL)),
            _resident((1, 2 * D_MODEL)),
            _resident((1, D_MODEL)), _resident((1, D_MODEL)),
            _resident((SGU_GROUPS, SGU_BLOCK, SGU_BLOCK)),
            _resident((SGU_BLOCK, D_MODEL)),
            _resident((D_MODEL, D_MODEL)),
        ] + ffn_specs,
        out_specs=row_spec,
        out_shape=jax.ShapeDtypeStruct((n_b, n_t, D_MODEL), F32),
        compiler_params=pltpu.CompilerParams(
            dimension_semantics=("arbitrary", "arbitrary"), vmem_limit_bytes=VMEM_LIMIT),
        name="sgu_ffn",
    )(x, norm_gains[1], sgu_w_in[0].astype(BF16), row2(sgu_b_in[0]),
      row2(sgu_ln_w[0]), row2(sgu_ln_b[0]), sgu_ws[0], bs_full, sgu_w_out[0].astype(BF16),
      ffn_w_gate[1].astype(BF16), ffn_w_up[1].astype(BF16), ffn_w_down[1].astype(BF16))
    return x
```

```python
import functools
import math

import jax
import jax.numpy as jnp
from jax import lax
from jax.experimental import pallas as pl
from jax.experimental.pallas import tpu as pltpu

F32 = jnp.float32
BF16 = jnp.bfloat16

D_MODEL = 1024
HEAD = 64
PAIR = 2 * HEAD
N_PAIRS = D_MODEL // PAIR
CHUNK = 64
SGU_BLOCK = 128
SGU_GROUPS = 8
D_FF = 2816
RMS_EPS = 1e-6
GN_EPS = 64e-5
LN_EPS = 1e-5

PRE_ROWS = 256
SCAN_ROWS = 512
SCAN_PAIRS = 2
P1_CHUNKS = 2
FFN_ROWS = 512
FFN_SLABS = ((0, 1536), (1536, D_FF))
VMEM_LIMIT = 56 * 1024 * 1024


def _dot(a, b):
    return jnp.dot(a.astype(BF16), b.astype(BF16), preferred_element_type=F32)


def _dot_nt(a, b):
    return lax.dot_general(a.astype(BF16), b.astype(BF16), (((1,), (1,)), ((), ())),
                           preferred_element_type=F32)


def _split2(x):
    hi = x.astype(BF16)
    lo = (x - hi.astype(F32)).astype(BF16)
    return hi, lo


def _rmsnorm(x, g):
    return x * lax.rsqrt(jnp.mean(x * x, axis=-1, keepdims=True) + RMS_EPS) * g


def _softplus(z):
    return jnp.maximum(z, 0.0) + jnp.log(1.0 + jnp.exp(-jnp.abs(z)))


def _gelu(x):
    return x * (lax.erf(x * (1.0 / math.sqrt(2.0))) + 1.0) * 0.5


def _ffn(h, wg_ref, wu_ref, wd_ref):
    hb = h.astype(BF16)
    acc = None
    for lo, hi in FFN_SLABS:
        sl = slice(lo, hi)
        gate = _dot(hb, wg_ref[:, sl])
        up = _dot(hb, wu_ref[:, sl])
        part = _dot(gate * jax.nn.sigmoid(gate) * up, wd_ref[sl, :])
        acc = part if acc is None else acc + part
    return acc


def _rwkv_pre_kernel(x_ref, xp_ref, gain_ref, mix_ref, wrkv_ref, w0_ref, w1_ref, w2_ref,
                     a0_ref, a1_ref, a2_ref, g1_ref, g2_ref, kk_ref, ka_ref, rk_ref,
                     e_ref, et_ref, tri_ref, sel_ref,
                     rt_ref, at_ref, kt_ref, bt_ref, v_ref, bonus_ref, g_ref, gl_ref):
    i = pl.program_id(1)
    gain = gain_ref[...]
    h = _rmsnorm(x_ref[0], gain)
    hp = _rmsnorm(xp_ref[0], gain)[7:8]
    hp = jnp.where(i > 0, hp, 0.0)
    row = lax.broadcasted_iota(jnp.int32, h.shape, 0)
    h_prev = jnp.where(row == 0, hp, pltpu.roll(h, 1, 0))
    h_b = h.astype(BF16)
    xx_b = (h_prev - h).astype(BF16)

    def shifted(c):
        return h_b + xx_b * mix_ref[c:c + 1, :].astype(BF16)

    r = _dot(shifted(0), wrkv_ref[0])
    k = _dot(shifted(1), wrkv_ref[1])
    v = _dot(shifted(2), wrkv_ref[2])
    w_raw = w0_ref[...] + _dot(jnp.tanh(_dot(shifted(3), w1_ref[...])), w2_ref[...])
    lw = -jnp.exp(-_softplus(-w_raw) - 0.5)
    a = jax.nn.sigmoid(a0_ref[...] + _dot(_dot(shifted(4), a1_ref[...]), a2_ref[...]))
    g_ref[0] = _dot(jax.nn.sigmoid(_dot(shifted(5), g1_ref[...])), g2_ref[...])

    def head_sum(q, parts):
        s = _dot(q, e_ref[...])
        return sum(_dot(p, et_ref[...]) for p in (_split2(s) if parts == 2 else (s,)))

    kkr = k * kk_ref[...]
    ss = head_sum(kkr * kkr, 2)
    kk = kkr * lax.rsqrt(jnp.maximum(ss, 1e-24))
    k2 = k * (1.0 + (a - 1.0) * ka_ref[...])
    bonus = head_sum(r * k2 * rk_ref[...], 1) * v
    avec = -kk
    bvec = kk * a

    parts = _split2(lw)
    c = sum(_dot(tri_ref[...], p) for p in parts)
    tot = sum(_dot(sel_ref[...], p) for p in parts)
    n_chunks = h.shape[0] // CHUNK
    e_neg = jnp.exp(-c)
    outs = (
        (rt_ref, r * jnp.exp(c)),
        (at_ref, avec * jnp.exp(c - lw)),
        (kt_ref, k2 * e_neg),
        (bt_ref, bvec * e_neg),
        (v_ref, v),
        (bonus_ref, bonus),
    )
    gl = jnp.exp(tot[:n_chunks])
    for p in range(N_PAIRS):
        sl = slice(p * PAIR, (p + 1) * PAIR)
        for ref, val in outs:
            ref[0, p] = val[:, sl].astype(ref.dtype)
        for j in range(n_chunks):
            gl_ref[0, p, j] = gl[j:j + 1, sl]


def _rwkv_scan_kernel(rt_ref, at_ref, kt_ref, bt_ref, v_ref, bonus_ref, gl_ref,
                      gnw_ref, gnb_ref, ones_ref, y_ref, s_ref, r2_ref, y0_ref, mlr_ref, nc_ref):
    n_batch, n_pp, rows = rt_ref.shape[0], rt_ref.shape[1], rt_ref.shape[2]
    seqs = [(b, q) for b in range(n_batch) for q in range(n_pp)]

    @pl.when(pl.program_id(1) == 0)
    def _():
        s_ref[...] = jnp.zeros_like(s_ref)

    lane = lax.broadcasted_iota(jnp.int32, (CHUNK, PAIR), 1)
    trow = lax.broadcasted_iota(jnp.int32, (CHUNK, PAIR), 0)
    tcol = lane & (HEAD - 1)
    first_head = lane < HEAD
    strict = tcol < trow
    incl = tcol <= trow
    same16 = (tcol >> 4) == (trow >> 4)
    same32 = (tcol >> 5) == (trow >> 5)
    eye = jnp.where(tcol == trow, 1.0, 0.0).astype(F32)
    r2 = lax.broadcasted_iota(jnp.int32, (PAIR, PAIR), 0)
    c2 = lax.broadcasted_iota(jnp.int32, (PAIR, PAIR), 1)
    same_head = (r2 >> 6) == (c2 >> 6)

    def bd(y):
        z = jnp.zeros_like(y)
        return jnp.concatenate([jnp.where(first_head, y, z), jnp.where(first_head, z, y)], axis=0)

    def pdot(x, y):
        return _dot(x, bd(y.astype(BF16)))

    cat = jnp.concatenate

    def chunk_rows(ci):
        start = ci * CHUNK
        return pl.ds(start if isinstance(ci, int) else pl.multiple_of(start, CHUNK), CHUNK)

    def phase1(it):
        where = [(b, q, n, it * P1_CHUNKS + cj)
                 for cj in range(P1_CHUNKS) for n, (b, q) in enumerate(seqs)]
        sls = [chunk_rows(ci) for _, _, _, ci in where]
        idx = range(len(where))
        rt = [rt_ref[where[i][0], where[i][1], sls[i], :] for i in idx]
        at = [at_ref[where[i][0], where[i][1], sls[i], :] for i in idx]
        kt = [kt_ref[where[i][0], where[i][1], sls[i], :] for i in idx]
        bt = [bt_ref[where[i][0], where[i][1], sls[i], :] for i in idx]
        vv = [v_ref[where[i][0], where[i][1], sls[i], :] for i in idx]
        bk = [cat([bt[i], kt[i]], axis=0).astype(F32) * gl_ref[where[i][0], where[i][1], where[i][3]]
              for i in idx]
        p_all = [_dot_nt(cat([at[i], rt[i]], axis=0), cat([bd(bt[i]), bd(kt[i])], axis=0)) for i in idx]
        a_ab = [jnp.where(strict, p[:CHUNK, :PAIR], 0.0) for p in p_all]
        a_ak = [jnp.where(strict, p[:CHUNK, PAIR:], 0.0) for p in p_all]
        a_rb = [jnp.where(incl, p[CHUNK:, :PAIR], 0.0) for p in p_all]
        a_rk = [jnp.where(incl, p[CHUNK:, PAIR:], 0.0) for p in p_all]
        av = [_dot(cat([a_ak[i], a_rk[i]], axis=0), bd(vv[i])) for i in idx]
        a_d = [jnp.where(same16, a, 0.0) for a in a_ab]
        t_m = [eye + a for a in a_d]
        p_m = [pdot(a, a) for a in a_d]
        for _ in range(2):
            res = [pdot(cat([p_m[i], t_m[i]], axis=0), p_m[i]) for i in idx]
            t_m = [t_m[i] + res[i][CHUNK:] for i in idx]
            p_m = [r[:CHUNK] for r in res]
        t_m = [t_m[i] + pdot(t_m[i], p_m[i]) for i in idx]
        for off in (jnp.logical_and(same32, jnp.logical_not(same16)), jnp.logical_not(same32)):
            x_m = [pdot(t_m[i], jnp.where(off, a_ab[i], 0.0)) for i in idx]
            t_m = [t_m[i] + pdot(x_m[i], t_m[i]) for i in idx]
        tx = [_dot(t_m[i], cat([bd(at[i]), bd(av[i][:CHUNK].astype(BF16))], axis=1)) for i in idx]
        at2 = [t[:, :PAIR] for t in tx]
        w_u = [t[:, PAIR:] for t in tx]
        e1 = [_dot(a_rb[i], cat([bd(at2[i].astype(BF16)), bd(w_u[i].astype(BF16))], axis=1)) for i in idx]
        zeros = jnp.zeros((CHUNK, PAIR), F32)
        lhs_t = [cat([cat([at2[i], zeros], axis=0).T, cat([w_u[i], vv[i].astype(F32)], axis=0).T], axis=0)
                 for i in idx]
        e2 = [_dot(lhs_t[i], bk[i]) for i in idx]
        for i in idx:
            _, _, n, ci = where[i]
            r2_ref[n, sls[i], :] = (rt[i].astype(F32) + e1[i][:, :PAIR]).astype(BF16)
            y0_ref[n, sls[i], :] = av[i][CHUNK:] + e1[i][:, PAIR:]
            mlr_ref[n, ci] = jnp.where(same_head, e2[i][:PAIR], 0.0).astype(BF16)
            nc_ref[n, ci] = jnp.where(same_head, e2[i][PAIR:], 0.0)

    def phase2(it):
        ns = range(len(seqs))
        state = [s_ref[n] for n in ns]
        for cj in range(P1_CHUNKS):
            ci = it * P1_CHUNKS + cj
            sl = chunk_rows(ci)
            s_bf = [s.astype(BF16) for s in state]
            y = [_dot_nt(r2_ref[n, sl, :], s_bf[n]) + y0_ref[n, sl, :] for n in ns]
            state = [state[n] * gl_ref[b, q, ci] + _dot(s_bf[n], mlr_ref[n, ci]) + nc_ref[n, ci]
                     for n, (b, q) in enumerate(seqs)]
            for n, (b, q) in enumerate(seqs):
                y_ref[b, q, sl, :] = y[n]
        for n in ns:
            s_ref[n] = state[n]

    n_groups = rows // (CHUNK * P1_CHUNKS)
    phase1(0)
    for it in range(n_groups - 1):
        phase2(it)
        phase1(it + 1)
    phase2(n_groups - 1)

    ones = ones_ref[...]

    def head_mean(z):
        return _dot(z, ones) * (1.0 / HEAD)

    for q in range(n_pp):
        y = y_ref[:, q].reshape(n_batch * rows, PAIR)
        d = y - head_mean(y)
        yn = d * lax.rsqrt(head_mean(d * d) + GN_EPS) * gnw_ref[q] + gnb_ref[q]
        y_ref[:, q] = yn.reshape(n_batch, rows, PAIR) + bonus_ref[:, q]


def _mix_ffn_kernel(x_ref, y_ref, g_ref, wo_ref, gains_ref, wg_ref, wu_ref, wd_ref, o_ref):
    y = jnp.concatenate([y_ref[0, p] for p in range(N_PAIRS)], axis=-1)
    m = _dot(y * g_ref[0], wo_ref[...])
    x1 = x_ref[0] + _rmsnorm(m, gains_ref[0:1])
    f = _ffn(_rmsnorm(x1, gains_ref[1:2]), wg_ref, wu_ref, wd_ref)
    o_ref[0] = x1 + _rmsnorm(f, gains_ref[2:3])


def _sgu_ffn_kernel(x_ref, gains_ref, win_ref, bin_ref, lnw_ref, lnb_ref, ws_ref, bs_ref,
                    wout_ref, wg_ref, wu_ref, wd_ref, o_ref):
    x = x_ref[0]
    rows = x.shape[0]
    hb = _rmsnorm(x, gains_ref[0:1]).astype(BF16)
    u = _gelu(_dot(hb, win_ref[:, :D_MODEL]) + bin_ref[:, :D_MODEL])
    v = _gelu(_dot(hb, win_ref[:, D_MODEL:]) + bin_ref[:, D_MODEL:])
    mu = jnp.mean(v, axis=-1, keepdims=True)
    dv = v - mu
    var = jnp.mean(dv * dv, axis=-1, keepdims=True)
    vn = (dv * lax.rsqrt(var + LN_EPS) * lnw_ref[...] + lnb_ref[...]).astype(BF16)

    ri = lax.broadcasted_iota(jnp.int32, (SGU_BLOCK, SGU_BLOCK), 0)
    ci = lax.broadcasted_iota(jnp.int32, (SGU_BLOCK, SGU_BLOCK), 1)
    causal = (ci // CHUNK) <= (ri // CHUNK)
    cols = []
    for gi in range(SGU_GROUPS):
        ws_m = jnp.where(causal, ws_ref[gi], 0.0).astype(BF16)
        lanes = slice(gi * SGU_BLOCK, (gi + 1) * SGU_BLOCK)
        blocks = [_dot(ws_m, vn[n * SGU_BLOCK:(n + 1) * SGU_BLOCK, lanes]) + bs_ref[:, lanes]
                  for n in range(rows // SGU_BLOCK)]
        cols.append(jnp.concatenate(blocks, axis=0))
    s = jnp.concatenate(cols, axis=-1)
    m = _dot(u * s, wout_ref[...])
    x1 = x + _rmsnorm(m, gains_ref[1:2])
    f = _ffn(_rmsnorm(x1, gains_ref[2:3]), wg_ref, wu_ref, wd_ref)
    o_ref[0] = x1 + _rmsnorm(f, gains_ref[3:4])


def _resident(shape):
    zeros = (0,) * len(shape)
    return pl.BlockSpec(shape, lambda *_: zeros, pipeline_mode=pl.Buffered(1))


def kernel(x, norm_gains, rwkv_mix, rwkv_w_rkv, rwkv_w0, rwkv_w1, rwkv_w2, rwkv_a0, rwkv_a1, rwkv_a2, rwkv_g1, rwkv_g2, rwkv_k_k, rwkv_k_a, rwkv_r_k, rwkv_gn_w, rwkv_gn_b, rwkv_w_o, sgu_w_in, sgu_b_in, sgu_ln_w, sgu_ln_b, sgu_ws, sgu_bs, sgu_w_out, ffn_w_gate, ffn_w_up, ffn_w_down):
    n_b, n_t, d = x.shape
    assert d == D_MODEL and n_t % SCAN_ROWS == 0 and n_t % FFN_ROWS == 0 and n_t % PRE_ROWS == 0
    row2 = lambda p: p.reshape(1, -1)

    ch = jnp.arange(D_MODEL) // HEAD
    e_mat = (ch[:, None] == jnp.arange(PAIR)[None, :]).astype(BF16)
    et_mat = e_mat.T
    tt = jnp.arange(PRE_ROWS)
    tri = jnp.logical_and(tt[:, None] // CHUNK == tt[None, :] // CHUNK,
                          tt[None, :] <= tt[:, None]).astype(BF16)
    sel = (jnp.arange(8)[:, None] == tt[None, :] // CHUNK).astype(BF16)
    pl_idx = jnp.arange(PAIR) // HEAD
    ones_blk = (pl_idx[:, None] == pl_idx[None, :]).astype(BF16)

    n_pre = n_t // PRE_ROWS
    slab = jax.ShapeDtypeStruct((n_b, N_PAIRS, n_t, PAIR), BF16)
    slab_spec = pl.BlockSpec((1, N_PAIRS, PRE_ROWS, PAIR), lambda b, i: (b, 0, i, 0))
    pre_out = pl.pallas_call(
        _rwkv_pre_kernel,
        grid=(n_b, n_pre),
        in_specs=[
            pl.BlockSpec((1, PRE_ROWS, D_MODEL), lambda b, i: (b, i, 0)),
            pl.BlockSpec((1, 8, D_MODEL), lambda b, i: (b, jnp.maximum(i * (PRE_ROWS // 8) - 1, 0), 0)),
            _resident((1, D_MODEL)),
            _resident((6, D_MODEL)),
            _resident((3, D_MODEL, D_MODEL)),
            _resident((1, D_MODEL)), _resident(rwkv_w1.shape[1:]), _resident(rwkv_w2.shape[1:]),
            _resident((1, D_MODEL)), _resident(rwkv_a1.shape[1:]), _resident(rwkv_a2.shape[1:]),
            _resident(rwkv_g1.shape[1:]), _resident(rwkv_g2.shape[1:]),
            _resident((1, D_MODEL)), _resident((1, D_MODEL)), _resident((1, D_MODEL)),
            _resident(e_mat.shape), _resident(et_mat.shape), _resident(tri.shape), _resident(sel.shape),
        ],
        out_specs=[slab_spec] * 6 + [
            pl.BlockSpec((1, PRE_ROWS, D_MODEL), lambda b, i: (b, i, 0)),
            pl.BlockSpec((1, N_PAIRS, PRE_ROWS // CHUNK, 1, PAIR), lambda b, i: (b, 0, i, 0, 0)),
        ],
        out_shape=[slab] * 6 + [
            jax.ShapeDtypeStruct((n_b, n_t, D_MODEL), F32),
            jax.ShapeDtypeStruct((n_b, N_PAIRS, n_t // CHUNK, 1, PAIR), F32),
        ],
        compiler_params=pltpu.CompilerParams(
            dimension_semantics=("arbitrary", "arbitrary"), vmem_limit_bytes=VMEM_LIMIT),
        name="rwkv_pre",
    )(x, x, row2(norm_gains[0, 0]), rwkv_mix[0], rwkv_w_rkv[0].astype(BF16),
      row2(rwkv_w0[0]), rwkv_w1[0].astype(BF16), rwkv_w2[0].astype(BF16),
      row2(rwkv_a0[0]), rwkv_a1[0].astype(BF16), rwkv_a2[0].astype(BF16),
      rwkv_g1[0].astype(BF16), rwkv_g2[0].astype(BF16),
      row2(rwkv_k_k[0]), row2(rwkv_k_a[0]), row2(rwkv_r_k[0]),
      e_mat, et_mat, tri, sel)
    rt, at, kt, bt, vv, bonus, gate, gl = pre_out

    scan_spec = pl.BlockSpec((n_b, SCAN_PAIRS, SCAN_ROWS, PAIR), lambda p, t: (0, p, t, 0))
    n_seq = n_b * SCAN_PAIRS
    y = pl.pallas_call(
        _rwkv_scan_kernel,
        grid=(N_PAIRS // SCAN_PAIRS, n_t // SCAN_ROWS),
        in_specs=[scan_spec] * 6 + [
            pl.BlockSpec((n_b, SCAN_PAIRS, SCAN_ROWS // CHUNK, 1, PAIR), lambda p, t: (0, p, t, 0, 0)),
            pl.BlockSpec((SCAN_PAIRS, 1, PAIR), lambda p, t: (p, 0, 0)),
            pl.BlockSpec((SCAN_PAIRS, 1, PAIR), lambda p, t: (p, 0, 0)),
            _resident(ones_blk.shape),
        ],
        out_specs=scan_spec,
        out_shape=jax.ShapeDtypeStruct((n_b, N_PAIRS, n_t, PAIR), F32),
        scratch_shapes=[pltpu.VMEM((n_seq, PAIR, PAIR), F32),
                        pltpu.VMEM((n_seq, SCAN_ROWS, PAIR), BF16),
                        pltpu.VMEM((n_seq, SCAN_ROWS, PAIR), F32),
                        pltpu.VMEM((n_seq, SCAN_ROWS // CHUNK, PAIR, PAIR), BF16),
                        pltpu.VMEM((n_seq, SCAN_ROWS // CHUNK, PAIR, PAIR), F32)],
        compiler_params=pltpu.CompilerParams(
            dimension_semantics=("arbitrary", "arbitrary"), vmem_limit_bytes=VMEM_LIMIT),
        name="rwkv_scan",
    )(rt, at, kt, bt, vv, bonus, gl,
      rwkv_gn_w[0].reshape(N_PAIRS, 1, PAIR), rwkv_gn_b[0].reshape(N_PAIRS, 1, PAIR), ones_blk)

    n_ffn = n_t // FFN_ROWS
    row_spec = pl.BlockSpec((1, FFN_ROWS, D_MODEL), lambda b, i: (b, i, 0))
    ffn_specs = [_resident((D_MODEL, D_FF)), _resident((D_MODEL, D_FF)), _resident((D_FF, D_MODEL))]
    x = pl.pallas_call(
        _mix_ffn_kernel,
        grid=(n_b, n_ffn),
        in_specs=[
            row_spec,
            pl.BlockSpec((1, N_PAIRS, FFN_ROWS, PAIR), lambda b, i: (b, 0, i, 0)),
            row_spec,
            _resident((D_MODEL, D_MODEL)),
            _resident((3, D_MODEL)),
        ] + ffn_specs,
        out_specs=row_spec,
        out_shape=jax.ShapeDtypeStruct((n_b, n_t, D_MODEL), F32),
        compiler_params=pltpu.CompilerParams(
            dimension_semantics=("arbitrary", "arbitrary"), vmem_limit_bytes=VMEM_LIMIT),
        name="mix_ffn",
    )(x, y, gate, rwkv_w_o[0].astype(BF16), norm_gains[0, 1:4],
      ffn_w_gate[0].astype(BF16), ffn_w_up[0].astype(BF16), ffn_w_down[0].astype(BF16))

    bs_full = jnp.repeat(sgu_bs[0].T, SGU_BLOCK, axis=1)
    x = pl.pallas_call(
        _sgu_ffn_kernel,
        grid=(n_b, n_ffn),
        in_specs=[
            row_spec,
            _resident((4, D_MODEL)),
            _resident((D_MODEL, 2 * D_MODEL)),
            _resident((1, 2 * D_MODEL)),
            _resident((1, D_MODEL)), _resident((1, D_MODEL)),
            _resident((SGU_GROUPS, SGU_BLOCK, SGU_BLOCK)),
            _resident((SGU_BLOCK, D_MODEL)),
            _resident((D_MODEL, D_MODEL)),
        ] + ffn_specs,
        out_specs=row_spec,
        out_shape=jax.ShapeDtypeStruct((n_b, n_t, D_MODEL), F32),
        compiler_params=pltpu.CompilerParams(
            dimension_semantics=("arbitrary", "arbitrary"), vmem_limit_bytes=VMEM_LIMIT),
        name="sgu_ffn",
    )(x, norm_gains[1], sgu_w_in[0].astype(BF16), row2(sgu_b_in[0]),
      row2(sgu_ln_w[0]), row2(sgu_ln_b[0]), sgu_ws[0], bs_full, sgu_w_out[0].astype(BF16),
      ffn_w_gate[1].astype(BF16), ffn_w_up[1].astype(BF16), ffn_w_down[1].astype(BF16))
    return x
```

```python
import functools
import math

import jax
import jax.numpy as jnp
from jax import lax
from jax.experimental import pallas as pl
from jax.experimental.pallas import tpu as pltpu

F32 = jnp.float32
BF16 = jnp.bfloat16

D_MODEL = 1024
HEAD = 64
PAIR = 2 * HEAD
N_PAIRS = D_MODEL // PAIR
CHUNK = 64
SGU_BLOCK = 128
SGU_GROUPS = 8
D_FF = 2816
RMS_EPS = 1e-6
GN_EPS = 64e-5
LN_EPS = 1e-5

PRE_ROWS = 512
TRI_ROWS = 256
SCAN_ROWS = 512
SCAN_PAIRS = 2
P1_CHUNKS = 2
FFN_ROWS = 512
FFN_SLABS = ((0, 1536), (1536, D_FF))
VMEM_LIMIT = 56 * 1024 * 1024


def _dot(a, b):
    return jnp.dot(a.astype(BF16), b.astype(BF16), preferred_element_type=F32)


def _dot_nt(a, b):
    return lax.dot_general(a.astype(BF16), b.astype(BF16), (((1,), (1,)), ((), ())),
                           preferred_element_type=F32)


def _split2(x):
    hi = x.astype(BF16)
    lo = (x - hi.astype(F32)).astype(BF16)
    return hi, lo


def _rmsnorm(x, g):
    return x * lax.rsqrt(jnp.mean(x * x, axis=-1, keepdims=True) + RMS_EPS) * g


def _softplus(z):
    return jnp.maximum(z, 0.0) + jnp.log(1.0 + jnp.exp(-jnp.abs(z)))


def _gelu(x):
    return x * (lax.erf(x * (1.0 / math.sqrt(2.0))) + 1.0) * 0.5


def _ffn(h, wg_ref, wu_ref, wd_ref):
    hb = h.astype(BF16)
    acc = None
    for lo, hi in FFN_SLABS:
        sl = slice(lo, hi)
        gate = _dot(hb, wg_ref[:, sl])
        up = _dot(hb, wu_ref[:, sl])
        part = _dot(gate * jax.nn.sigmoid(gate) * up, wd_ref[sl, :])
        acc = part if acc is None else acc + part
    return acc


def _rwkv_pre_kernel(x_ref, xp_ref, gain_ref, mix_ref, wrkv_ref, w0_ref, w1_ref, w2_ref,
                     a0_ref, a1_ref, a2_ref, g1_ref, g2_ref, kk_ref, ka_ref, rk_ref,
                     e_ref, et_ref, tri_ref, sel_ref,
                     rt_ref, at_ref, kt_ref, bt_ref, v_ref, bonus_ref, g_ref, gl_ref):
    i = pl.program_id(1)
    gain = gain_ref[...]
    h = _rmsnorm(x_ref[0], gain)
    hp = _rmsnorm(xp_ref[0], gain)[7:8]
    hp = jnp.where(i > 0, hp, 0.0)
    row = lax.broadcasted_iota(jnp.int32, h.shape, 0)
    h_prev = jnp.where(row == 0, hp, pltpu.roll(h, 1, 0))
    h_b = h.astype(BF16)
    xx_b = (h_prev - h).astype(BF16)

    def shifted(c):
        return h_b + xx_b * mix_ref[c:c + 1, :].astype(BF16)

    r = _dot(shifted(0), wrkv_ref[0])
    k = _dot(shifted(1), wrkv_ref[1])
    v = _dot(shifted(2), wrkv_ref[2])
    w_raw = w0_ref[...] + _dot(jnp.tanh(_dot(shifted(3), w1_ref[...])), w2_ref[...])
    lw = -jnp.exp(-_softplus(-w_raw) - 0.5)
    a = jax.nn.sigmoid(a0_ref[...] + _dot(_dot(shifted(4), a1_ref[...]), a2_ref[...]))
    g_ref[0] = _dot(jax.nn.sigmoid(_dot(shifted(5), g1_ref[...])), g2_ref[...])

    def head_sum(q, parts):
        s = _dot(q, e_ref[...])
        return sum(_dot(p, et_ref[...]) for p in (_split2(s) if parts == 2 else (s,)))

    kkr = k * kk_ref[...]
    ss = head_sum(kkr * kkr, 2)
    kk = kkr * lax.rsqrt(jnp.maximum(ss, 1e-24))
    k2 = k * (1.0 + (a - 1.0) * ka_ref[...])
    bonus = head_sum(r * k2 * rk_ref[...], 1) * v
    avec = -kk
    bvec = kk * a

    parts = _split2(lw)
    n_chunks = h.shape[0] // CHUNK
    blocks = [slice(j, j + TRI_ROWS) for j in range(0, h.shape[0], TRI_ROWS)]
    c = jnp.concatenate([sum(_dot(tri_ref[...], p[blk]) for p in parts) for blk in blocks], axis=0)
    tot = jnp.concatenate([sum(_dot(sel_ref[...], p[blk]) for p in parts)[:TRI_ROWS // CHUNK]
                           for blk in blocks], axis=0)
    e_neg = jnp.exp(-c)
    outs = (
        (rt_ref, r * jnp.exp(c)),
        (at_ref, avec * jnp.exp(c - lw)),
        (kt_ref, k2 * e_neg),
        (bt_ref, bvec * e_neg),
        (v_ref, v),
        (bonus_ref, bonus),
    )
    gl = jnp.exp(tot)
    for p in range(N_PAIRS):
        sl = slice(p * PAIR, (p + 1) * PAIR)
        for ref, val in outs:
            ref[0, p] = val[:, sl].astype(ref.dtype)
        for j in range(n_chunks):
            gl_ref[0, p, j] = gl[j:j + 1, sl]


def _rwkv_scan_kernel(rt_ref, at_ref, kt_ref, bt_ref, v_ref, bonus_ref, gl_ref,
                      gnw_ref, gnb_ref, ones_ref, y_ref, s_ref, r2_ref, y0_ref, mlr_ref, nc_ref):
    n_batch, n_pp, rows = rt_ref.shape[0], rt_ref.shape[1], rt_ref.shape[2]
    seqs = [(b, q) for b in range(n_batch) for q in range(n_pp)]

    @pl.when(pl.program_id(1) == 0)
    def _():
        s_ref[...] = jnp.zeros_like(s_ref)

    lane = lax.broadcasted_iota(jnp.int32, (CHUNK, PAIR), 1)
    trow = lax.broadcasted_iota(jnp.int32, (CHUNK, PAIR), 0)
    tcol = lane & (HEAD - 1)
    first_head = lane < HEAD
    strict = tcol < trow
    incl = tcol <= trow
    same8, same16, same32 = [(tcol >> s) == (trow >> s) for s in (3, 4, 5)]
    merge_masks = (jnp.logical_and(same16, jnp.logical_not(same8)),
                   jnp.logical_and(same32, jnp.logical_not(same16)),
                   jnp.logical_not(same32))
    eye = jnp.where(tcol == trow, 1.0, 0.0).astype(F32)
    r2 = lax.broadcasted_iota(jnp.int32, (PAIR, PAIR), 0)
    c2 = lax.broadcasted_iota(jnp.int32, (PAIR, PAIR), 1)
    same_head = (r2 >> 6) == (c2 >> 6)

    def bd(y):
        z = jnp.zeros_like(y)
        return jnp.concatenate([jnp.where(first_head, y, z), jnp.where(first_head, z, y)], axis=0)

    def pdot(x, y):
        return _dot(x, bd(y.astype(BF16)))

    cat = jnp.concatenate

    def chunk_rows(ci):
        start = ci * CHUNK
        return pl.ds(start if isinstance(ci, int) else pl.multiple_of(start, CHUNK), CHUNK)

    def phase1(it):
        where = [(b, q, n, it * P1_CHUNKS + cj)
                 for cj in range(P1_CHUNKS) for n, (b, q) in enumerate(seqs)]
        sls = [chunk_rows(ci) for _, _, _, ci in where]
        idx = range(len(where))
        rt = [rt_ref[where[i][0], where[i][1], sls[i], :] for i in idx]
        at = [at_ref[where[i][0], where[i][1], sls[i], :] for i in idx]
        kt = [kt_ref[where[i][0], where[i][1], sls[i], :] for i in idx]
        bt = [bt_ref[where[i][0], where[i][1], sls[i], :] for i in idx]
        vv = [v_ref[where[i][0], where[i][1], sls[i], :] for i in idx]
        bk = [cat([bt[i], kt[i]], axis=0).astype(F32) * gl_ref[where[i][0], where[i][1], where[i][3]]
              for i in idx]
        p_all = [_dot_nt(cat([at[i], rt[i]], axis=0), cat([bd(bt[i]), bd(kt[i])], axis=0)) for i in idx]
        a_ab = [jnp.where(strict, p[:CHUNK, :PAIR], 0.0) for p in p_all]
        a_ak = [jnp.where(strict, p[:CHUNK, PAIR:], 0.0) for p in p_all]
        a_rb = [jnp.where(incl, p[CHUNK:, :PAIR], 0.0) for p in p_all]
        a_rk = [jnp.where(incl, p[CHUNK:, PAIR:], 0.0) for p in p_all]
        av = [_dot(cat([a_ak[i], a_rk[i]], axis=0), bd(vv[i])) for i in idx]
        a_d = [jnp.where(same8, a, 0.0) for a in a_ab]
        t_m = [eye + a for a in a_d]
        p_m = [pdot(a, a) for a in a_d]
        res = [pdot(cat([p_m[i], t_m[i]], axis=0), p_m[i]) for i in idx]
        t_m = [t_m[i] + res[i][CHUNK:] for i in idx]
        t_m = [t_m[i] + pdot(t_m[i], res[i][:CHUNK]) for i in idx]
        for off in merge_masks:
            x_m = [pdot(t_m[i], jnp.where(off, a_ab[i], 0.0)) for i in idx]
            t_m = [t_m[i] + pdot(x_m[i], t_m[i]) for i in idx]
        tx = [_dot(t_m[i], cat([bd(at[i]), bd(av[i][:CHUNK].astype(BF16))], axis=1)) for i in idx]
        at2 = [t[:, :PAIR] for t in tx]
        w_u = [t[:, PAIR:] for t in tx]
        e1 = [_dot(a_rb[i], cat([bd(at2[i].astype(BF16)), bd(w_u[i].astype(BF16))], axis=1)) for i in idx]
        zeros = jnp.zeros((CHUNK, PAIR), F32)
        lhs_t = [cat([cat([at2[i], zeros], axis=0).T, cat([w_u[i], vv[i].astype(F32)], axis=0).T], axis=0)
                 for i in idx]
        e2 = [_dot(lhs_t[i], bk[i]) for i in idx]
        for i in idx:
            _, _, n, ci = where[i]
            r2_ref[n, sls[i], :] = (rt[i].astype(F32) + e1[i][:, :PAIR]).astype(BF16)
            y0_ref[n, sls[i], :] = av[i][CHUNK:] + e1[i][:, PAIR:]
            mlr_ref[n, ci] = jnp.where(same_head, e2[i][:PAIR], 0.0).astype(BF16)
            nc_ref[n, ci] = jnp.where(same_head, e2[i][PAIR:], 0.0)

    def phase2(it):
        ns = range(len(seqs))
        state = [s_ref[n] for n in ns]
        for cj in range(P1_CHUNKS):
            ci = it * P1_CHUNKS + cj
            sl = chunk_rows(ci)
            s_bf = [s.astype(BF16) for s in state]
            y = [_dot_nt(r2_ref[n, sl, :], s_bf[n]) + y0_ref[n, sl, :] for n in ns]
            state = [state[n] * gl_ref[b, q, ci] + _dot(s_bf[n], mlr_ref[n, ci]) + nc_ref[n, ci]
                     for n, (b, q) in enumerate(seqs)]
            for n, (b, q) in enumerate(seqs):
                y_ref[b, q, sl, :] = y[n]
        for n in ns:
            s_ref[n] = state[n]

    n_groups = rows // (CHUNK * P1_CHUNKS)
    phase1(0)
    for it in range(n_groups - 1):
        phase2(it)
        phase1(it + 1)
    phase2(n_groups - 1)

    ones = ones_ref[...]

    def head_mean(z):
        return _dot(z, ones) * (1.0 / HEAD)

    for q in range(n_pp):
        y = y_ref[:, q].reshape(n_batch * rows, PAIR)
        d = y - head_mean(y)
        yn = d * lax.rsqrt(head_mean(d * d) + GN_EPS) * gnw_ref[q] + gnb_ref[q]
        y_ref[:, q] = yn.reshape(n_batch, rows, PAIR) + bonus_ref[:, q]


def _mix_ffn_kernel(x_ref, y_ref, g_ref, wo_ref, gains_ref, wg_ref, wu_ref, wd_ref, o_ref):
    y = jnp.concatenate([y_ref[0, p] for p in range(N_PAIRS)], axis=-1)
    m = _dot(y * g_ref[0], wo_ref[...])
    x1 = x_ref[0] + _rmsnorm(m, gains_ref[0:1])
    f = _ffn(_rmsnorm(x1, gains_ref[1:2]), wg_ref, wu_ref, wd_ref)
    o_ref[0] = x1 + _rmsnorm(f, gains_ref[2:3])


def _sgu_ffn_kernel(x_ref, gains_ref, win_ref, bin_ref, lnw_ref, lnb_ref, ws_ref, bs_ref,
                    wout_ref, wg_ref, wu_ref, wd_ref, o_ref):
    x = x_ref[0]
    rows = x.shape[0]
    hb = _rmsnorm(x, gains_ref[0:1]).astype(BF16)
    u = _gelu(_dot(hb, win_ref[:, :D_MODEL]) + bin_ref[:, :D_MODEL])
    v = _gelu(_dot(hb, win_ref[:, D_MODEL:]) + bin_ref[:, D_MODEL:])
    mu = jnp.mean(v, axis=-1, keepdims=True)
    dv = v - mu
    var = jnp.mean(dv * dv, axis=-1, keepdims=True)
    vn = (dv * lax.rsqrt(var + LN_EPS) * lnw_ref[...] + lnb_ref[...]).astype(BF16)

    ri = lax.broadcasted_iota(jnp.int32, (SGU_BLOCK, SGU_BLOCK), 0)
    ci = lax.broadcasted_iota(jnp.int32, (SGU_BLOCK, SGU_BLOCK), 1)
    causal = (ci // CHUNK) <= (ri // CHUNK)
    cols = []
    for gi in range(SGU_GROUPS):
        ws_m = jnp.where(causal, ws_ref[gi], 0.0).astype(BF16)
        lanes = slice(gi * SGU_BLOCK, (gi + 1) * SGU_BLOCK)
        blocks = [_dot(ws_m, vn[n * SGU_BLOCK:(n + 1) * SGU_BLOCK, lanes]) + bs_ref[:, lanes]
                  for n in range(rows // SGU_BLOCK)]
        cols.append(jnp.concatenate(blocks, axis=0))
    s = jnp.concatenate(cols, axis=-1)
    m = _dot(u * s, wout_ref[...])
    x1 = x + _rmsnorm(m, gains_ref[1:2])
    f = _ffn(_rmsnorm(x1, gains_ref[2:3]), wg_ref, wu_ref, wd_ref)
    o_ref[0] = x1 + _rmsnorm(f, gains_ref[3:4])


def _resident(shape):
    zeros = (0,) * len(shape)
    return pl.BlockSpec(shape, lambda *_: zeros, pipeline_mode=pl.Buffered(1))


def kernel(x, norm_gains, rwkv_mix, rwkv_w_rkv, rwkv_w0, rwkv_w1, rwkv_w2, rwkv_a0, rwkv_a1, rwkv_a2, rwkv_g1, rwkv_g2, rwkv_k_k, rwkv_k_a, rwkv_r_k, rwkv_gn_w, rwkv_gn_b, rwkv_w_o, sgu_w_in, sgu_b_in, sgu_ln_w, sgu_ln_b, sgu_ws, sgu_bs, sgu_w_out, ffn_w_gate, ffn_w_up, ffn_w_down):
    n_b, n_t, d = x.shape
    assert d == D_MODEL and n_t % SCAN_ROWS == 0 and n_t % FFN_ROWS == 0 and n_t % PRE_ROWS == 0
    row2 = lambda p: p.reshape(1, -1)

    ch = jnp.arange(D_MODEL) // HEAD
    e_mat = (ch[:, None] == jnp.arange(PAIR)[None, :]).astype(BF16)
    et_mat = e_mat.T
    tt = jnp.arange(TRI_ROWS)
    tri = jnp.logical_and(tt[:, None] // CHUNK == tt[None, :] // CHUNK,
                          tt[None, :] <= tt[:, None]).astype(BF16)
    sel = (jnp.arange(8)[:, None] == tt[None, :] // CHUNK).astype(BF16)
    pl_idx = jnp.arange(PAIR) // HEAD
    ones_blk = (pl_idx[:, None] == pl_idx[None, :]).astype(BF16)

    n_pre = n_t // PRE_ROWS
    slab = jax.ShapeDtypeStruct((n_b, N_PAIRS, n_t, PAIR), BF16)
    slab_spec = pl.BlockSpec((1, N_PAIRS, PRE_ROWS, PAIR), lambda b, i: (b, 0, i, 0))
    pre_out = pl.pallas_call(
        _rwkv_pre_kernel,
        grid=(n_b, n_pre),
        in_specs=[
            pl.BlockSpec((1, PRE_ROWS, D_MODEL), lambda b, i: (b, i, 0)),
            pl.BlockSpec((1, 8, D_MODEL), lambda b, i: (b, jnp.maximum(i * (PRE_ROWS // 8) - 1, 0), 0)),
            _resident((1, D_MODEL)),
            _resident((6, D_MODEL)),
            _resident((3, D_MODEL, D_MODEL)),
            _resident((1, D_MODEL)), _resident(rwkv_w1.shape[1:]), _resident(rwkv_w2.shape[1:]),
            _resident((1, D_MODEL)), _resident(rwkv_a1.shape[1:]), _resident(rwkv_a2.shape[1:]),
            _resident(rwkv_g1.shape[1:]), _resident(rwkv_g2.shape[1:]),
            _resident((1, D_MODEL)), _resident((1, D_MODEL)), _resident((1, D_MODEL)),
            _resident(e_mat.shape), _resident(et_mat.shape), _resident(tri.shape), _resident(sel.shape),
        ],
        out_specs=[slab_spec] * 6 + [
            pl.BlockSpec((1, PRE_ROWS, D_MODEL), lambda b, i: (b, i, 0)),
            pl.BlockSpec((1, N_PAIRS, PRE_ROWS // CHUNK, 1, PAIR), lambda b, i: (b, 0, i, 0, 0)),
        ],
        out_shape=[slab] * 6 + [
            jax.ShapeDtypeStruct((n_b, n_t, D_MODEL), F32),
            jax.ShapeDtypeStruct((n_b, N_PAIRS, n_t // CHUNK, 1, PAIR), F32),
        ],
        compiler_params=pltpu.CompilerParams(
            dimension_semantics=("arbitrary", "arbitrary"), vmem_limit_bytes=VMEM_LIMIT),
        name="rwkv_pre",
    )(x, x, row2(norm_gains[0, 0]), rwkv_mix[0], rwkv_w_rkv[0].astype(BF16),
      row2(rwkv_w0[0]), rwkv_w1[0].astype(BF16), rwkv_w2[0].astype(BF16),
      row2(rwkv_a0[0]), rwkv_a1[0].astype(BF16), rwkv_a2[0].astype(BF16),
      rwkv_g1[0].astype(BF16), rwkv_g2[0].astype(BF16),
      row2(rwkv_k_k[0]), row2(rwkv_k_a[0]), row2(rwkv_r_k[0]),
      e_mat, et_mat, tri, sel)
    rt, at, kt, bt, vv, bonus, gate, gl = pre_out

    scan_spec = pl.BlockSpec((n_b, SCAN_PAIRS, SCAN_ROWS, PAIR), lambda p, t: (0, p, t, 0))
    n_seq = n_b * SCAN_PAIRS
    y = pl.pallas_call(
        _rwkv_scan_kernel,
        grid=(N_PAIRS // SCAN_PAIRS, n_t // SCAN_ROWS),
        in_specs=[scan_spec] * 6 + [
            pl.BlockSpec((n_b, SCAN_PAIRS, SCAN_ROWS // CHUNK, 1, PAIR), lambda p, t: (0, p, t, 0, 0)),
            pl.BlockSpec((SCAN_PAIRS, 1, PAIR), lambda p, t: (p, 0, 0)),
            pl.BlockSpec((SCAN_PAIRS, 1, PAIR), lambda p, t: (p, 0, 0)),
            _resident(ones_blk.shape),
        ],
        out_specs=scan_spec,
        out_shape=jax.ShapeDtypeStruct((n_b, N_PAIRS, n_t, PAIR), F32),
        scratch_shapes=[pltpu.VMEM((n_seq, PAIR, PAIR), F32),
                        pltpu.VMEM((n_seq, SCAN_ROWS, PAIR), BF16),
                        pltpu.VMEM((n_seq, SCAN_ROWS, PAIR), F32),
                        pltpu.VMEM((n_seq, SCAN_ROWS // CHUNK, PAIR, PAIR), BF16),
                        pltpu.VMEM((n_seq, SCAN_ROWS // CHUNK, PAIR, PAIR), F32)],
        compiler_params=pltpu.CompilerParams(
            dimension_semantics=("arbitrary", "arbitrary"), vmem_limit_bytes=VMEM_LIMIT),
        name="rwkv_scan",
    )(rt, at, kt, bt, vv, bonus, gl,
      rwkv_gn_w[0].reshape(N_PAIRS, 1, PAIR), rwkv_gn_b[0].reshape(N_PAIRS, 1, PAIR), ones_blk)

    n_ffn = n_t // FFN_ROWS
    row_spec = pl.BlockSpec((1, FFN_ROWS, D_MODEL), lambda b, i: (b, i, 0))
    ffn_specs = [_resident((D_MODEL, D_FF)), _resident((D_MODEL, D_FF)), _resident((D_FF, D_MODEL))]
    x = pl.pallas_call(
        _mix_ffn_kernel,
        grid=(n_b, n_ffn),
        in_specs=[
            row_spec,
            pl.BlockSpec((1, N_PAIRS, FFN_ROWS, PAIR), lambda b, i: (b, 0, i, 0)),
            row_spec,
            _resident((D_MODEL, D_MODEL)),
            _resident((3, D_MODEL)),
        ] + ffn_specs,
        out_specs=row_spec,
        out_shape=jax.ShapeDtypeStruct((n_b, n_t, D_MODEL), F32),
        compiler_params=pltpu.CompilerParams(
            dimension_semantics=("arbitrary", "arbitrary"), vmem_limit_bytes=VMEM_LIMIT),
        name="mix_ffn",
    )(x, y, gate, rwkv_w_o[0].astype(BF16), norm_gains[0, 1:4],
      ffn_w_gate[0].astype(BF16), ffn_w_up[0].astype(BF16), ffn_w_down[0].astype(BF16))

    bs_full = jnp.repeat(sgu_bs[0].T, SGU_BLOCK, axis=1)
    x = pl.pallas_call(
        _sgu_ffn_kernel,
        grid=(n_b, n_ffn),
        in_specs=[
            row_spec,
            _resident((4, D_MODEL)),
            _resident((D_MODEL, 2 * D_MODEL)),
            _resident((1, 2 * D_MODEL)),
            _resident((1, D_MODEL)), _resident((1, D_MODEL)),
            _resident((SGU_GROUPS, SGU_BLOCK, SGU_BLOCK)),
            _resident((SGU_BLOCK, D_MODEL)),
            _resident((D_MODEL, D_MODEL)),
        ] + ffn_specs,
        out_specs=row_spec,
        out_shape=jax.ShapeDtypeStruct((n_b, n_t, D_MODEL), F32),
        compiler_params=pltpu.CompilerParams(
            dimension_semantics=("arbitrary", "arbitrary"), vmem_limit_bytes=VMEM_LIMIT),
        name="sgu_ffn",
    )(x, norm_gains[1], sgu_w_in[0].astype(BF16), row2(sgu_b_in[0]),
      row2(sgu_ln_w[0]), row2(sgu_ln_b[0]), sgu_ws[0], bs_full, sgu_w_out[0].astype(BF16),
      ffn_w_gate[1].astype(BF16), ffn_w_up[1].astype(BF16), ffn_w_down[1].astype(BF16))
    return x
```

```python
import math

import jax
import jax.numpy as jnp
import numpy as np
from jax import lax
from jax.experimental import pallas as pl
from jax.experimental.pallas import tpu as pltpu

F32 = jnp.float32
BF16 = jnp.bfloat16

D_MODEL = 1024
HEAD = 64
PAIR = 2 * HEAD
N_PAIRS = D_MODEL // PAIR
CHUNK = 64
SGU_BLOCK = 128
SGU_GROUPS = 8
D_FF = 2816
RMS_EPS = 1e-6
GN_EPS = 64e-5
LN_EPS = 1e-5

PRE_ROWS = 512
TRI_ROWS = 256
SCAN_ROWS = 512
SCAN_PAIRS = 2
P1_CHUNKS = 2
FFN_ROWS = 512
FFN_SLABS = ((0, 1536), (1536, D_FF))
VMEM_LIMIT = 56 * 1024 * 1024


def _dot(a, b):
    return jnp.dot(a.astype(BF16), b.astype(BF16), preferred_element_type=F32)


def _dot_nt(a, b):
    return lax.dot_general(a.astype(BF16), b.astype(BF16), (((1,), (1,)), ((), ())),
                           preferred_element_type=F32)


def _split2(x):
    hi = x.astype(BF16)
    lo = (x - hi.astype(F32)).astype(BF16)
    return hi, lo


def _rmsnorm(x, g):
    return x * lax.rsqrt(jnp.mean(x * x, axis=-1, keepdims=True) + RMS_EPS) * g


def _softplus(z):
    return jnp.maximum(z, 0.0) + jnp.log(1.0 + jnp.exp(-jnp.abs(z)))


def _gelu(x):
    return x * (lax.erf(x * (1.0 / math.sqrt(2.0))) + 1.0) * 0.5


def _ffn(h, wg_ref, wu_ref, wd_ref):
    hb = h.astype(BF16)
    acc = None
    for lo, hi in FFN_SLABS:
        sl = slice(lo, hi)
        gate = _dot(hb, wg_ref[:, sl])
        up = _dot(hb, wu_ref[:, sl])
        part = _dot(gate * jax.nn.sigmoid(gate) * up, wd_ref[sl, :])
        acc = part if acc is None else acc + part
    return acc


def _rwkv_pre_kernel(x_ref, xp_ref, gain_ref, mix_ref, wrkv_ref, w0_ref, w1_ref, w2_ref,
                     a0_ref, a1_ref, a2_ref, g1_ref, g2_ref, kk_ref, ka_ref, rk_ref,
                     e_ref, et_ref, tri_ref, sel_ref,
                     rt_ref, at_ref, kt_ref, bt_ref, v_ref, bonus_ref, g_ref, gl_ref):
    i = pl.program_id(1)
    gain = gain_ref[0:1]
    h = _rmsnorm(x_ref[0], gain)
    hp = _rmsnorm(xp_ref[0], gain)[7:8]
    hp = jnp.where(i > 0, hp, 0.0)
    row = lax.broadcasted_iota(jnp.int32, h.shape, 0)
    h_prev = jnp.where(row == 0, hp, pltpu.roll(h, 1, 0))
    h_b = h.astype(BF16)
    xx_b = (h_prev - h).astype(BF16)

    def shifted(c):
        return h_b + xx_b * mix_ref[c:c + 1, :].astype(BF16)

    r = _dot(shifted(0), wrkv_ref[0])
    k = _dot(shifted(1), wrkv_ref[1])
    v = _dot(shifted(2), wrkv_ref[2])
    w_raw = w0_ref[...] + _dot(jnp.tanh(_dot(shifted(3), w1_ref[...])), w2_ref[...])
    lw = -jnp.exp(-_softplus(-w_raw) - 0.5)
    a = jax.nn.sigmoid(a0_ref[...] + _dot(_dot(shifted(4), a1_ref[...]), a2_ref[...]))
    g_ref[0] = _dot(jax.nn.sigmoid(_dot(shifted(5), g1_ref[...])), g2_ref[...])

    def head_sum(q, parts):
        s = _dot(q, e_ref[...])
        return sum(_dot(p, et_ref[...]) for p in (_split2(s) if parts == 2 else (s,)))

    kkr = k * kk_ref[...]
    ss = head_sum(kkr * kkr, 2)
    kk = kkr * lax.rsqrt(jnp.maximum(ss, 1e-24))
    k2 = k * (1.0 + (a - 1.0) * ka_ref[...])
    bonus = head_sum(r * k2 * rk_ref[...], 1) * v
    avec = -kk
    bvec = kk * a

    parts = _split2(lw)
    n_chunks = h.shape[0] // CHUNK
    blocks = [slice(j, j + TRI_ROWS) for j in range(0, h.shape[0], TRI_ROWS)]
    c = jnp.concatenate([sum(_dot(tri_ref[...], p[blk]) for p in parts) for blk in blocks], axis=0)
    tot = jnp.concatenate([sum(_dot(sel_ref[...], p[blk]) for p in parts)[:TRI_ROWS // CHUNK]
                           for blk in blocks], axis=0)
    e_neg = jnp.exp(-c)
    outs = (
        (rt_ref, r * jnp.exp(c)),
        (at_ref, avec * jnp.exp(c - lw)),
        (kt_ref, k2 * e_neg),
        (bt_ref, bvec * e_neg),
        (v_ref, v),
        (bonus_ref, bonus),
    )
    gl = jnp.exp(tot)
    for p in range(N_PAIRS):
        sl = slice(p * PAIR, (p + 1) * PAIR)
        for ref, val in outs:
            ref[0, p] = val[:, sl].astype(ref.dtype)
        for j in range(n_chunks):
            gl_ref[0, p, j] = gl[j:j + 1, sl]


def _rwkv_scan_kernel(rt_ref, at_ref, kt_ref, bt_ref, v_ref, bonus_ref, gl_ref,
                      gnw_ref, gnb_ref, ones_ref, y_ref, s_ref, r2_ref, y0_ref, mlr_ref, nc_ref):
    n_batch, n_pp, rows = rt_ref.shape[0], rt_ref.shape[1], rt_ref.shape[2]
    seqs = [(b, q) for b in range(n_batch) for q in range(n_pp)]

    @pl.when(pl.program_id(1) == 0)
    def _():
        s_ref[...] = jnp.zeros_like(s_ref)

    lane = lax.broadcasted_iota(jnp.int32, (CHUNK, PAIR), 1)
    trow = lax.broadcasted_iota(jnp.int32, (CHUNK, PAIR), 0)
    tcol = lane & (HEAD - 1)
    first_head = lane < HEAD
    strict = tcol < trow
    incl = tcol <= trow
    same8, same16, same32 = [(tcol >> s) == (trow >> s) for s in (3, 4, 5)]
    merge_masks = (jnp.logical_and(same16, jnp.logical_not(same8)),
                   jnp.logical_and(same32, jnp.logical_not(same16)),
                   jnp.logical_not(same32))
    eye = jnp.where(tcol == trow, 1.0, 0.0).astype(F32)
    r2 = lax.broadcasted_iota(jnp.int32, (PAIR, PAIR), 0)
    c2 = lax.broadcasted_iota(jnp.int32, (PAIR, PAIR), 1)
    same_head = (r2 >> 6) == (c2 >> 6)

    def bd(y):
        z = jnp.zeros_like(y)
        return jnp.concatenate([jnp.where(first_head, y, z), jnp.where(first_head, z, y)], axis=0)

    def pdot(x, y):
        return _dot(x, bd(y.astype(BF16)))

    cat = jnp.concatenate

    def chunk_rows(ci):
        start = ci * CHUNK
        return pl.ds(start if isinstance(ci, int) else pl.multiple_of(start, CHUNK), CHUNK)

    def phase1(it):
        where = [(b, q, n, it * P1_CHUNKS + cj)
                 for cj in range(P1_CHUNKS) for n, (b, q) in enumerate(seqs)]
        sls = [chunk_rows(ci) for _, _, _, ci in where]
        idx = range(len(where))
        rt = [rt_ref[where[i][0], where[i][1], sls[i], :] for i in idx]
        at = [at_ref[where[i][0], where[i][1], sls[i], :] for i in idx]
        kt = [kt_ref[where[i][0], where[i][1], sls[i], :] for i in idx]
        bt = [bt_ref[where[i][0], where[i][1], sls[i], :] for i in idx]
        vv = [v_ref[where[i][0], where[i][1], sls[i], :] for i in idx]
        bk = [cat([bt[i], kt[i]], axis=0).astype(F32) * gl_ref[where[i][0], where[i][1], where[i][3]]
              for i in idx]
        p_all = [_dot_nt(cat([at[i], rt[i]], axis=0), cat([bd(bt[i]), bd(kt[i])], axis=0)) for i in idx]
        a_ab = [jnp.where(strict, p[:CHUNK, :PAIR], 0.0) for p in p_all]
        a_ak = [jnp.where(strict, p[:CHUNK, PAIR:], 0.0) for p in p_all]
        a_rb = [jnp.where(incl, p[CHUNK:, :PAIR], 0.0) for p in p_all]
        a_rk = [jnp.where(incl, p[CHUNK:, PAIR:], 0.0) for p in p_all]
        av = [_dot(cat([a_ak[i], a_rk[i]], axis=0), bd(vv[i])) for i in idx]
        a_d = [jnp.where(same8, a, 0.0) for a in a_ab]
        t_m = [eye + a for a in a_d]
        p_m = [pdot(a, a) for a in a_d]
        res = [pdot(cat([p_m[i], t_m[i]], axis=0), p_m[i]) for i in idx]
        t_m = [t_m[i] + res[i][CHUNK:] for i in idx]
        t_m = [t_m[i] + pdot(t_m[i], res[i][:CHUNK]) for i in idx]
        for off in merge_masks:
            x_m = [pdot(t_m[i], jnp.where(off, a_ab[i], 0.0)) for i in idx]
            t_m = [t_m[i] + pdot(x_m[i], t_m[i]) for i in idx]
        tx = [_dot(t_m[i], cat([bd(at[i]), bd(av[i][:CHUNK].astype(BF16))], axis=1)) for i in idx]
        at2 = [t[:, :PAIR] for t in tx]
        w_u = [t[:, PAIR:] for t in tx]
        e1 = [_dot(a_rb[i], cat([bd(at2[i].astype(BF16)), bd(w_u[i].astype(BF16))], axis=1)) for i in idx]
        zeros = jnp.zeros((CHUNK, PAIR), F32)
        lhs_t = [cat([cat([at2[i], zeros], axis=0).T, cat([w_u[i], vv[i].astype(F32)], axis=0).T], axis=0)
                 for i in idx]
        e2 = [_dot(lhs_t[i], bk[i]) for i in idx]
        for i in idx:
            _, _, n, ci = where[i]
            r2_ref[n, sls[i], :] = (rt[i].astype(F32) + e1[i][:, :PAIR]).astype(BF16)
            y0_ref[n, sls[i], :] = av[i][CHUNK:] + e1[i][:, PAIR:]
            mlr_ref[n, ci] = jnp.where(same_head, e2[i][:PAIR], 0.0).astype(BF16)
            nc_ref[n, ci] = jnp.where(same_head, e2[i][PAIR:], 0.0)

    def phase2(it):
        ns = range(len(seqs))
        state = [s_ref[n] for n in ns]
        for cj in range(P1_CHUNKS):
            ci = it * P1_CHUNKS + cj
            sl = chunk_rows(ci)
            s_bf = [s.astype(BF16) for s in state]
            y = [_dot_nt(r2_ref[n, sl, :], s_bf[n]) + y0_ref[n, sl, :] for n in ns]
            state = [state[n] * gl_ref[b, q, ci] + _dot(s_bf[n], mlr_ref[n, ci]) + nc_ref[n, ci]
                     for n, (b, q) in enumerate(seqs)]
            for n, (b, q) in enumerate(seqs):
                y_ref[b, q, sl, :] = y[n]
        for n in ns:
            s_ref[n] = state[n]

    n_groups = rows // (CHUNK * P1_CHUNKS)
    phase1(0)
    for it in range(n_groups - 1):
        phase2(it)
        phase1(it + 1)
    phase2(n_groups - 1)

    ones = ones_ref[...]

    def head_mean(z):
        return _dot(z, ones) * (1.0 / HEAD)

    for q in range(n_pp):
        y = y_ref[:, q].reshape(n_batch * rows, PAIR)
        d = y - head_mean(y)
        yn = d * lax.rsqrt(head_mean(d * d) + GN_EPS) * gnw_ref[q] + gnb_ref[q]
        y_ref[:, q] = yn.reshape(n_batch, rows, PAIR) + bonus_ref[:, q]


def _mix_ffn_kernel(x_ref, y_ref, g_ref, wo_ref, gains_ref, wg_ref, wu_ref, wd_ref, o_ref):
    y = jnp.concatenate([y_ref[0, p] for p in range(N_PAIRS)], axis=-1)
    m = _dot(y * g_ref[0], wo_ref[...])
    x1 = x_ref[0] + _rmsnorm(m, gains_ref[1:2])
    f = _ffn(_rmsnorm(x1, gains_ref[2:3]), wg_ref, wu_ref, wd_ref)
    o_ref[0] = x1 + _rmsnorm(f, gains_ref[3:4])


def _sgu_ffn_kernel(x_ref, gains_ref, win_ref, bin_ref, lnw_ref, lnb_ref, ws_ref, bs_ref,
                    wout_ref, wg_ref, wu_ref, wd_ref, o_ref):
    x = x_ref[0]
    rows = x.shape[0]
    hb = _rmsnorm(x, gains_ref[4:5]).astype(BF16)
    u = _gelu(_dot(hb, win_ref[:, :D_MODEL]) + bin_ref[:, :D_MODEL])
    v = _gelu(_dot(hb, win_ref[:, D_MODEL:]) + bin_ref[:, D_MODEL:])
    mu = jnp.mean(v, axis=-1, keepdims=True)
    dv = v - mu
    var = jnp.mean(dv * dv, axis=-1, keepdims=True)
    vn = (dv * lax.rsqrt(var + LN_EPS) * lnw_ref[...] + lnb_ref[...]).astype(BF16)

    ri = lax.broadcasted_iota(jnp.int32, (SGU_BLOCK, SGU_BLOCK), 0)
    ci = lax.broadcasted_iota(jnp.int32, (SGU_BLOCK, SGU_BLOCK), 1)
    causal = (ci // CHUNK) <= (ri // CHUNK)
    cols = []
    for gi in range(SGU_GROUPS):
        ws_m = jnp.where(causal, ws_ref[gi], 0.0).astype(BF16)
        lanes = slice(gi * SGU_BLOCK, (gi + 1) * SGU_BLOCK)
        blocks = [_dot(ws_m, vn[n * SGU_BLOCK:(n + 1) * SGU_BLOCK, lanes]) + bs_ref[:, lanes]
                  for n in range(rows // SGU_BLOCK)]
        cols.append(jnp.concatenate(blocks, axis=0))
    s = jnp.concatenate(cols, axis=-1)
    m = _dot(u * s, wout_ref[...])
    x1 = x + _rmsnorm(m, gains_ref[5:6])
    f = _ffn(_rmsnorm(x1, gains_ref[6:7]), wg_ref, wu_ref, wd_ref)
    o_ref[0] = x1 + _rmsnorm(f, gains_ref[7:8])


def _resident(shape):
    zeros = (0,) * len(shape)
    return pl.BlockSpec(shape, lambda *_: zeros, pipeline_mode=pl.Buffered(1))


def kernel(x, norm_gains, rwkv_mix, rwkv_w_rkv, rwkv_w0, rwkv_w1, rwkv_w2, rwkv_a0, rwkv_a1, rwkv_a2, rwkv_g1, rwkv_g2, rwkv_k_k, rwkv_k_a, rwkv_r_k, rwkv_gn_w, rwkv_gn_b, rwkv_w_o, sgu_w_in, sgu_b_in, sgu_ln_w, sgu_ln_b, sgu_ws, sgu_bs, sgu_w_out, ffn_w_gate, ffn_w_up, ffn_w_down):
    n_b, n_t, d = x.shape
    assert d == D_MODEL and n_t % SCAN_ROWS == 0 and n_t % FFN_ROWS == 0 and n_t % PRE_ROWS == 0
    row2 = lambda p: p.reshape(1, -1)

    ch = np.arange(D_MODEL) // HEAD
    e_mat = np.asarray(ch[:, None] == np.arange(PAIR)[None, :], dtype=BF16)
    et_mat = np.ascontiguousarray(e_mat.T)
    tt = np.arange(TRI_ROWS)
    tri = np.asarray((tt[:, None] // CHUNK == tt[None, :] // CHUNK) & (tt[None, :] <= tt[:, None]),
                     dtype=BF16)
    sel = np.asarray(np.arange(8)[:, None] == tt[None, :] // CHUNK, dtype=BF16)
    pl_idx = np.arange(PAIR) // HEAD
    ones_blk = np.asarray(pl_idx[:, None] == pl_idx[None, :], dtype=BF16)
    gains = norm_gains.reshape(-1, D_MODEL)
    w_gate, w_up, w_down = (w.astype(BF16) for w in (ffn_w_gate, ffn_w_up, ffn_w_down))

    n_pre = n_t // PRE_ROWS
    slab = jax.ShapeDtypeStruct((n_b, N_PAIRS, n_t, PAIR), BF16)
    slab_spec = pl.BlockSpec((1, N_PAIRS, PRE_ROWS, PAIR), lambda b, i: (b, 0, i, 0))
    pre_out = pl.pallas_call(
        _rwkv_pre_kernel,
        grid=(n_b, n_pre),
        in_specs=[
            pl.BlockSpec((1, PRE_ROWS, D_MODEL), lambda b, i: (b, i, 0)),
            pl.BlockSpec((1, 8, D_MODEL), lambda b, i: (b, jnp.maximum(i * (PRE_ROWS // 8) - 1, 0), 0)),
            _resident((8, D_MODEL)),
            _resident((6, D_MODEL)),
            _resident((3, D_MODEL, D_MODEL)),
            _resident((1, D_MODEL)), _resident(rwkv_w1.shape[1:]), _resident(rwkv_w2.shape[1:]),
            _resident((1, D_MODEL)), _resident(rwkv_a1.shape[1:]), _resident(rwkv_a2.shape[1:]),
            _resident(rwkv_g1.shape[1:]), _resident(rwkv_g2.shape[1:]),
            _resident((1, D_MODEL)), _resident((1, D_MODEL)), _resident((1, D_MODEL)),
            _resident(e_mat.shape), _resident(et_mat.shape), _resident(tri.shape), _resident(sel.shape),
        ],
        out_specs=[slab_spec] * 6 + [
            pl.BlockSpec((1, PRE_ROWS, D_MODEL), lambda b, i: (b, i, 0)),
            pl.BlockSpec((1, N_PAIRS, PRE_ROWS // CHUNK, 1, PAIR), lambda b, i: (b, 0, i, 0, 0)),
        ],
        out_shape=[slab] * 6 + [
            jax.ShapeDtypeStruct((n_b, n_t, D_MODEL), F32),
            jax.ShapeDtypeStruct((n_b, N_PAIRS, n_t // CHUNK, 1, PAIR), F32),
        ],
        compiler_params=pltpu.CompilerParams(
            dimension_semantics=("arbitrary", "arbitrary"), vmem_limit_bytes=VMEM_LIMIT),
        name="rwkv_pre",
    )(x, x, gains, rwkv_mix[0], rwkv_w_rkv[0].astype(BF16),
      row2(rwkv_w0[0]), rwkv_w1[0], rwkv_w2[0], row2(rwkv_a0[0]), rwkv_a1[0], rwkv_a2[0],
      rwkv_g1[0], rwkv_g2[0],
      row2(rwkv_k_k[0]), row2(rwkv_k_a[0]), row2(rwkv_r_k[0]),
      e_mat, et_mat, tri, sel)
    rt, at, kt, bt, vv, bonus, gate, gl = pre_out

    scan_spec = pl.BlockSpec((n_b, SCAN_PAIRS, SCAN_ROWS, PAIR), lambda p, t: (0, p, t, 0))
    n_seq = n_b * SCAN_PAIRS
    y = pl.pallas_call(
        _rwkv_scan_kernel,
        grid=(N_PAIRS // SCAN_PAIRS, n_t // SCAN_ROWS),
        in_specs=[scan_spec] * 6 + [
            pl.BlockSpec((n_b, SCAN_PAIRS, SCAN_ROWS // CHUNK, 1, PAIR), lambda p, t: (0, p, t, 0, 0)),
            pl.BlockSpec((SCAN_PAIRS, 1, PAIR), lambda p, t: (p, 0, 0)),
            pl.BlockSpec((SCAN_PAIRS, 1, PAIR), lambda p, t: (p, 0, 0)),
            _resident(ones_blk.shape),
        ],
        out_specs=scan_spec,
        out_shape=jax.ShapeDtypeStruct((n_b, N_PAIRS, n_t, PAIR), F32),
        scratch_shapes=[pltpu.VMEM((n_seq, PAIR, PAIR), F32),
                        pltpu.VMEM((n_seq, SCAN_ROWS, PAIR), BF16),
                        pltpu.VMEM((n_seq, SCAN_ROWS, PAIR), F32),
                        pltpu.VMEM((n_seq, SCAN_ROWS // CHUNK, PAIR, PAIR), BF16),
                        pltpu.VMEM((n_seq, SCAN_ROWS // CHUNK, PAIR, PAIR), F32)],
        compiler_params=pltpu.CompilerParams(
            dimension_semantics=("arbitrary", "arbitrary"), vmem_limit_bytes=VMEM_LIMIT),
        name="rwkv_scan",
    )(rt, at, kt, bt, vv, bonus, gl,
      rwkv_gn_w[0].reshape(N_PAIRS, 1, PAIR), rwkv_gn_b[0].reshape(N_PAIRS, 1, PAIR), ones_blk)

    n_ffn = n_t // FFN_ROWS
    row_spec = pl.BlockSpec((1, FFN_ROWS, D_MODEL), lambda b, i: (b, i, 0))

    def ffn_specs(layer):
        pick = lambda *_: (layer, 0, 0)
        return [pl.BlockSpec((None,) + shape, pick, pipeline_mode=pl.Buffered(1))
                for shape in ((D_MODEL, D_FF), (D_MODEL, D_FF), (D_FF, D_MODEL))]

    x = pl.pallas_call(
        _mix_ffn_kernel,
        grid=(n_b, n_ffn),
        in_specs=[
            row_spec,
            pl.BlockSpec((1, N_PAIRS, FFN_ROWS, PAIR), lambda b, i: (b, 0, i, 0)),
            row_spec,
            _resident((D_MODEL, D_MODEL)),
            _resident((8, D_MODEL)),
        ] + ffn_specs(0),
        out_specs=row_spec,
        out_shape=jax.ShapeDtypeStruct((n_b, n_t, D_MODEL), F32),
        compiler_params=pltpu.CompilerParams(
            dimension_semantics=("arbitrary", "arbitrary"), vmem_limit_bytes=VMEM_LIMIT),
        name="mix_ffn",
    )(x, y, gate, rwkv_w_o[0].astype(BF16), gains, w_gate, w_up, w_down)

    bs_full = jnp.repeat(sgu_bs[0].T, SGU_BLOCK, axis=1)
    x = pl.pallas_call(
        _sgu_ffn_kernel,
        grid=(n_b, n_ffn),
        in_specs=[
            row_spec,
            _resident((8, D_MODEL)),
            _resident((D_MODEL, 2 * D_MODEL)),
            _resident((1, 2 * D_MODEL)),
            _resident((1, D_MODEL)), _resident((1, D_MODEL)),
            _resident((SGU_GROUPS, SGU_BLOCK, SGU_BLOCK)),
            _resident((SGU_BLOCK, D_MODEL)),
            _resident((D_MODEL, D_MODEL)),
        ] + ffn_specs(1),
        out_specs=row_spec,
        out_shape=jax.ShapeDtypeStruct((n_b, n_t, D_MODEL), F32),
        compiler_params=pltpu.CompilerParams(
            dimension_semantics=("arbitrary", "arbitrary"), vmem_limit_bytes=VMEM_LIMIT),
        name="sgu_ffn",
    )(x, gains, sgu_w_in[0].astype(BF16), row2(sgu_b_in[0]),
      row2(sgu_ln_w[0]), row2(sgu_ln_b[0]), sgu_ws[0], bs_full, sgu_w_out[0].astype(BF16),
      w_gate, w_up, w_down)
    return x
```

```python
import math

import jax
import jax.numpy as jnp
import numpy as np
from jax import lax
from jax.experimental import pallas as pl
from jax.experimental.pallas import tpu as pltpu

F32 = jnp.float32
BF16 = jnp.bfloat16

D_MODEL = 1024
HEAD = 64
PAIR = 2 * HEAD
N_PAIRS = D_MODEL // PAIR
CHUNK = 64
SGU_BLOCK = 128
SGU_GROUPS = 8
D_FF = 2816
RMS_EPS = 1e-6
GN_EPS = 64e-5
LN_EPS = 1e-5

PRE_ROWS = 512
TRI_ROWS = 256
SCAN_ROWS = 512
SCAN_PAIRS = 2
P1_CHUNKS = 2
FFN_ROWS = 512
SUB_ROWS = 256
FFN_SLABS = ((0, 1536), (1536, D_FF))
VMEM_LIMIT = 56 * 1024 * 1024


def _dot(a, b):
    return jnp.dot(a.astype(BF16), b.astype(BF16), preferred_element_type=F32)


def _dot_nt(a, b):
    return lax.dot_general(a.astype(BF16), b.astype(BF16), (((1,), (1,)), ((), ())),
                           preferred_element_type=F32)


def _split2(x):
    hi = x.astype(BF16)
    lo = (x - hi.astype(F32)).astype(BF16)
    return hi, lo


def _rmsnorm(x, g):
    return x * lax.rsqrt(jnp.mean(x * x, axis=-1, keepdims=True) + RMS_EPS) * g


def _softplus(z):
    return jnp.maximum(z, 0.0) + jnp.log(1.0 + jnp.exp(-jnp.abs(z)))


def _gelu(x):
    return x * (lax.erf(x * (1.0 / math.sqrt(2.0))) + 1.0) * 0.5


def _ffn(hs, wg_ref, wu_ref, wd_ref):
    hb = [h.astype(BF16) for h in hs]
    acc = [None] * len(hs)
    for lo, hi in FFN_SLABS:
        sl = slice(lo, hi)
        gate = [_dot(h, wg_ref[:, sl]) for h in hb]
        up = [_dot(h, wu_ref[:, sl]) for h in hb]
        part = [_dot(g * jax.nn.sigmoid(g) * u, wd_ref[sl, :]) for g, u in zip(gate, up)]
        acc = [p if a is None else a + p for a, p in zip(acc, part)]
    return acc


def _sub_tiles(rows):
    return [slice(j, j + SUB_ROWS) for j in range(0, rows, SUB_ROWS)]


def _rwkv_pre_kernel(x_ref, xp_ref, gain_ref, mix_ref, wrkv_ref, w0_ref, w1_ref, w2_ref,
                     a0_ref, a1_ref, a2_ref, g1_ref, g2_ref, kk_ref, ka_ref, rk_ref,
                     e_ref, et_ref, tri_ref, sel_ref,
                     rt_ref, at_ref, kt_ref, bt_ref, v_ref, bonus_ref, g_ref, gl_ref):
    i = pl.program_id(1)
    gain = gain_ref[0:1]
    h = _rmsnorm(x_ref[0], gain)
    hp = _rmsnorm(xp_ref[0], gain)[7:8]
    hp = jnp.where(i > 0, hp, 0.0)
    row = lax.broadcasted_iota(jnp.int32, h.shape, 0)
    h_prev = jnp.where(row == 0, hp, pltpu.roll(h, 1, 0))
    h_b = h.astype(BF16)
    xx_b = (h_prev - h).astype(BF16)

    def shifted(c):
        return h_b + xx_b * mix_ref[c:c + 1, :].astype(BF16)

    r = _dot(shifted(0), wrkv_ref[0])
    k = _dot(shifted(1), wrkv_ref[1])
    v = _dot(shifted(2), wrkv_ref[2])
    w_raw = w0_ref[...] + _dot(jnp.tanh(_dot(shifted(3), w1_ref[...])), w2_ref[...])
    lw = -jnp.exp(-_softplus(-w_raw) - 0.5)
    a = jax.nn.sigmoid(a0_ref[...] + _dot(_dot(shifted(4), a1_ref[...]), a2_ref[...]))
    g_ref[0] = _dot(jax.nn.sigmoid(_dot(shifted(5), g1_ref[...])), g2_ref[...])

    def head_sum(q, parts):
        s = _dot(q, e_ref[...])
        return sum(_dot(p, et_ref[...]) for p in (_split2(s) if parts == 2 else (s,)))

    kkr = k * kk_ref[...]
    ss = head_sum(kkr * kkr, 2)
    kk = kkr * lax.rsqrt(jnp.maximum(ss, 1e-24))
    k2 = k * (1.0 + (a - 1.0) * ka_ref[...])
    bonus = head_sum(r * k2 * rk_ref[...], 1) * v
    avec = -kk
    bvec = kk * a

    parts = _split2(lw)
    n_chunks = h.shape[0] // CHUNK
    blocks = [slice(j, j + TRI_ROWS) for j in range(0, h.shape[0], TRI_ROWS)]
    c = jnp.concatenate([sum(_dot(tri_ref[...], p[blk]) for p in parts) for blk in blocks], axis=0)
    tot = jnp.concatenate([sum(_dot(sel_ref[...], p[blk]) for p in parts)[:TRI_ROWS // CHUNK]
                           for blk in blocks], axis=0)
    e_neg = jnp.exp(-c)
    outs = (
        (rt_ref, r * jnp.exp(c)),
        (at_ref, avec * jnp.exp(c - lw)),
        (kt_ref, k2 * e_neg),
        (bt_ref, bvec * e_neg),
        (v_ref, v),
        (bonus_ref, bonus),
    )
    gl = jnp.exp(tot)
    for p in range(N_PAIRS):
        sl = slice(p * PAIR, (p + 1) * PAIR)
        for ref, val in outs:
            ref[0, p] = val[:, sl].astype(ref.dtype)
        for j in range(n_chunks):
            gl_ref[0, p, j] = gl[j:j + 1, sl]


def _rwkv_scan_kernel(rt_ref, at_ref, kt_ref, bt_ref, v_ref, bonus_ref, gl_ref,
                      gnw_ref, gnb_ref, ones_ref, y_ref, s_ref, r2_ref, y0_ref, mlr_ref, nc_ref):
    n_batch, n_pp, rows = rt_ref.shape[0], rt_ref.shape[1], rt_ref.shape[2]
    seqs = [(b, q) for b in range(n_batch) for q in range(n_pp)]

    @pl.when(pl.program_id(1) == 0)
    def _():
        s_ref[...] = jnp.zeros_like(s_ref)

    lane = lax.broadcasted_iota(jnp.int32, (CHUNK, PAIR), 1)
    trow = lax.broadcasted_iota(jnp.int32, (CHUNK, PAIR), 0)
    tcol = lane & (HEAD - 1)
    first_head = lane < HEAD
    strict = tcol < trow
    incl = tcol <= trow
    same8, same16, same32 = [(tcol >> s) == (trow >> s) for s in (3, 4, 5)]
    merge_masks = (jnp.logical_and(same16, jnp.logical_not(same8)),
                   jnp.logical_and(same32, jnp.logical_not(same16)),
                   jnp.logical_not(same32))
    eye = jnp.where(tcol == trow, 1.0, 0.0).astype(F32)
    r2 = lax.broadcasted_iota(jnp.int32, (PAIR, PAIR), 0)
    c2 = lax.broadcasted_iota(jnp.int32, (PAIR, PAIR), 1)
    same_head = (r2 >> 6) == (c2 >> 6)

    def bd(y):
        z = jnp.zeros_like(y)
        return jnp.concatenate([jnp.where(first_head, y, z), jnp.where(first_head, z, y)], axis=0)

    def pdot(x, y):
        return _dot(x, bd(y.astype(BF16)))

    cat = jnp.concatenate

    def chunk_rows(ci):
        start = ci * CHUNK
        return pl.ds(start if isinstance(ci, int) else pl.multiple_of(start, CHUNK), CHUNK)

    def phase1(it):
        where = [(b, q, n, it * P1_CHUNKS + cj)
                 for cj in range(P1_CHUNKS) for n, (b, q) in enumerate(seqs)]
        sls = [chunk_rows(ci) for _, _, _, ci in where]
        idx = range(len(where))
        rt = [rt_ref[where[i][0], where[i][1], sls[i], :] for i in idx]
        at = [at_ref[where[i][0], where[i][1], sls[i], :] for i in idx]
        kt = [kt_ref[where[i][0], where[i][1], sls[i], :] for i in idx]
        bt = [bt_ref[where[i][0], where[i][1], sls[i], :] for i in idx]
        vv = [v_ref[where[i][0], where[i][1], sls[i], :] for i in idx]
        bk = [cat([bt[i], kt[i]], axis=0).astype(F32) * gl_ref[where[i][0], where[i][1], where[i][3]]
              for i in idx]
        p_all = [_dot_nt(cat([at[i], rt[i]], axis=0), cat([bd(bt[i]), bd(kt[i])], axis=0)) for i in idx]
        a_ab = [jnp.where(strict, p[:CHUNK, :PAIR], 0.0) for p in p_all]
        a_ak = [jnp.where(strict, p[:CHUNK, PAIR:], 0.0) for p in p_all]
        a_rb = [jnp.where(incl, p[CHUNK:, :PAIR], 0.0) for p in p_all]
        a_rk = [jnp.where(incl, p[CHUNK:, PAIR:], 0.0) for p in p_all]
        av = [_dot(cat([a_ak[i], a_rk[i]], axis=0), bd(vv[i])) for i in idx]
        a_d = [jnp.where(same8, a, 0.0) for a in a_ab]
        t_m = [eye + a for a in a_d]
        p_m = [pdot(a, a) for a in a_d]
        res = [pdot(cat([p_m[i], t_m[i]], axis=0), p_m[i]) for i in idx]
        t_m = [t_m[i] + res[i][CHUNK:] for i in idx]
        t_m = [t_m[i] + pdot(t_m[i], res[i][:CHUNK]) for i in idx]
        for off in merge_masks:
            x_m = [pdot(t_m[i], jnp.where(off, a_ab[i], 0.0)) for i in idx]
            t_m = [t_m[i] + pdot(x_m[i], t_m[i]) for i in idx]
        tx = [_dot(t_m[i], cat([bd(at[i]), bd(av[i][:CHUNK].astype(BF16))], axis=1)) for i in idx]
        at2 = [t[:, :PAIR] for t in tx]
        w_u = [t[:, PAIR:] for t in tx]
        e1 = [_dot(a_rb[i], cat([bd(at2[i].astype(BF16)), bd(w_u[i].astype(BF16))], axis=1)) for i in idx]
        zeros = jnp.zeros((CHUNK, PAIR), F32)
        lhs_t = [cat([cat([at2[i], zeros], axis=0).T, cat([w_u[i], vv[i].astype(F32)], axis=0).T], axis=0)
                 for i in idx]
        e2 = [_dot(lhs_t[i], bk[i]) for i in idx]
        for i in idx:
            _, _, n, ci = where[i]
            r2_ref[n, sls[i], :] = (rt[i].astype(F32) + e1[i][:, :PAIR]).astype(BF16)
            y0_ref[n, sls[i], :] = av[i][CHUNK:] + e1[i][:, PAIR:]
            mlr_ref[n, ci] = jnp.where(same_head, e2[i][:PAIR], 0.0).astype(BF16)
            nc_ref[n, ci] = jnp.where(same_head, e2[i][PAIR:], 0.0)

    def phase2(it):
        ns = range(len(seqs))
        state = [s_ref[n] for n in ns]
        for cj in range(P1_CHUNKS):
            ci = it * P1_CHUNKS + cj
            sl = chunk_rows(ci)
            s_bf = [s.astype(BF16) for s in state]
            y = [_dot_nt(r2_ref[n, sl, :], s_bf[n]) + y0_ref[n, sl, :] for n in ns]
            state = [state[n] * gl_ref[b, q, ci] + _dot(s_bf[n], mlr_ref[n, ci]) + nc_ref[n, ci]
                     for n, (b, q) in enumerate(seqs)]
            for n, (b, q) in enumerate(seqs):
                y_ref[b, q, sl, :] = y[n]
        for n in ns:
            s_ref[n] = state[n]

    n_groups = rows // (CHUNK * P1_CHUNKS)
    phase1(0)
    for it in range(n_groups - 1):
        phase2(it)
        phase1(it + 1)
    phase2(n_groups - 1)

    ones = ones_ref[...]

    def head_mean(z):
        return _dot(z, ones) * (1.0 / HEAD)

    for q in range(n_pp):
        y = y_ref[:, q].reshape(n_batch * rows, PAIR)
        d = y - head_mean(y)
        yn = d * lax.rsqrt(head_mean(d * d) + GN_EPS) * gnw_ref[q] + gnb_ref[q]
        y_ref[:, q] = yn.reshape(n_batch, rows, PAIR) + bonus_ref[:, q]


def _mix_ffn_kernel(x_ref, y_ref, g_ref, wo_ref, gains_ref, wg_ref, wu_ref, wd_ref, o_ref):
    subs = _sub_tiles(x_ref.shape[1])
    y = [jnp.concatenate([y_ref[0, p, sl, :] for p in range(N_PAIRS)], axis=-1) for sl in subs]
    m = [_dot(yy * g_ref[0, sl, :], wo_ref[...]) for yy, sl in zip(y, subs)]
    x1 = [x_ref[0, sl, :] + _rmsnorm(mm, gains_ref[1:2]) for mm, sl in zip(m, subs)]
    f = _ffn([_rmsnorm(xx, gains_ref[2:3]) for xx in x1], wg_ref, wu_ref, wd_ref)
    for sl, xx, ff in zip(subs, x1, f):
        o_ref[0, sl, :] = xx + _rmsnorm(ff, gains_ref[3:4])


def _sgu_ffn_kernel(x_ref, gains_ref, win_ref, bin_ref, lnw_ref, lnb_ref, ws_ref, bs_ref,
                    wout_ref, wg_ref, wu_ref, wd_ref, o_ref):
    subs = _sub_tiles(x_ref.shape[1])
    x = [x_ref[0, sl, :] for sl in subs]
    hb = [_rmsnorm(xx, gains_ref[4:5]).astype(BF16) for xx in x]
    u = [_gelu(_dot(h, win_ref[:, :D_MODEL]) + bin_ref[:, :D_MODEL]) for h in hb]
    v = [_gelu(_dot(h, win_ref[:, D_MODEL:]) + bin_ref[:, D_MODEL:]) for h in hb]

    def layer_norm(vv):
        dv = vv - jnp.mean(vv, axis=-1, keepdims=True)
        var = jnp.mean(dv * dv, axis=-1, keepdims=True)
        return (dv * lax.rsqrt(var + LN_EPS) * lnw_ref[...] + lnb_ref[...]).astype(BF16)

    vn = [layer_norm(vv) for vv in v]
    ri = lax.broadcasted_iota(jnp.int32, (SGU_BLOCK, SGU_BLOCK), 0)
    ci = lax.broadcasted_iota(jnp.int32, (SGU_BLOCK, SGU_BLOCK), 1)
    causal = (ci // CHUNK) <= (ri // CHUNK)
    ws_m = [jnp.where(causal, ws_ref[gi], 0.0).astype(BF16) for gi in range(SGU_GROUPS)]

    def spatial(vv):
        cols = []
        for gi in range(SGU_GROUPS):
            lanes = slice(gi * SGU_BLOCK, (gi + 1) * SGU_BLOCK)
            blocks = [_dot(ws_m[gi], vv[n * SGU_BLOCK:(n + 1) * SGU_BLOCK, lanes]) + bs_ref[:, lanes]
                      for n in range(vv.shape[0] // SGU_BLOCK)]
            cols.append(jnp.concatenate(blocks, axis=0))
        return jnp.concatenate(cols, axis=-1)

    m = [_dot(uu * spatial(vv), wout_ref[...]) for uu, vv in zip(u, vn)]
    x1 = [xx + _rmsnorm(mm, gains_ref[5:6]) for xx, mm in zip(x, m)]
    f = _ffn([_rmsnorm(xx, gains_ref[6:7]) for xx in x1], wg_ref, wu_ref, wd_ref)
    for sl, xx, ff in zip(subs, x1, f):
        o_ref[0, sl, :] = xx + _rmsnorm(ff, gains_ref[7:8])


def _resident(shape):
    zeros = (0,) * len(shape)
    return pl.BlockSpec(shape, lambda *_: zeros, pipeline_mode=pl.Buffered(1))


def kernel(x, norm_gains, rwkv_mix, rwkv_w_rkv, rwkv_w0, rwkv_w1, rwkv_w2, rwkv_a0, rwkv_a1, rwkv_a2, rwkv_g1, rwkv_g2, rwkv_k_k, rwkv_k_a, rwkv_r_k, rwkv_gn_w, rwkv_gn_b, rwkv_w_o, sgu_w_in, sgu_b_in, sgu_ln_w, sgu_ln_b, sgu_ws, sgu_bs, sgu_w_out, ffn_w_gate, ffn_w_up, ffn_w_down):
    n_b, n_t, d = x.shape
    assert d == D_MODEL and n_t % SCAN_ROWS == 0 and n_t % FFN_ROWS == 0 and n_t % PRE_ROWS == 0
    row2 = lambda p: p.reshape(1, -1)

    ch = np.arange(D_MODEL) // HEAD
    e_mat = np.asarray(ch[:, None] == np.arange(PAIR)[None, :], dtype=BF16)
    et_mat = np.ascontiguousarray(e_mat.T)
    tt = np.arange(TRI_ROWS)
    tri = np.asarray((tt[:, None] // CHUNK == tt[None, :] // CHUNK) & (tt[None, :] <= tt[:, None]),
                     dtype=BF16)
    sel = np.asarray(np.arange(8)[:, None] == tt[None, :] // CHUNK, dtype=BF16)
    pl_idx = np.arange(PAIR) // HEAD
    ones_blk = np.asarray(pl_idx[:, None] == pl_idx[None, :], dtype=BF16)
    gains = norm_gains.reshape(-1, D_MODEL)
    w_gate, w_up, w_down = (w.astype(BF16) for w in (ffn_w_gate, ffn_w_up, ffn_w_down))

    n_pre = n_t // PRE_ROWS
    slab = jax.ShapeDtypeStruct((n_b, N_PAIRS, n_t, PAIR), BF16)
    slab_spec = pl.BlockSpec((1, N_PAIRS, PRE_ROWS, PAIR), lambda b, i: (b, 0, i, 0))
    pre_out = pl.pallas_call(
        _rwkv_pre_kernel,
        grid=(n_b, n_pre),
        in_specs=[
            pl.BlockSpec((1, PRE_ROWS, D_MODEL), lambda b, i: (b, i, 0)),
            pl.BlockSpec((1, 8, D_MODEL), lambda b, i: (b, jnp.maximum(i * (PRE_ROWS // 8) - 1, 0), 0)),
            _resident((8, D_MODEL)),
            _resident((6, D_MODEL)),
            _resident((3, D_MODEL, D_MODEL)),
            _resident((1, D_MODEL)), _resident(rwkv_w1.shape[1:]), _resident(rwkv_w2.shape[1:]),
            _resident((1, D_MODEL)), _resident(rwkv_a1.shape[1:]), _resident(rwkv_a2.shape[1:]),
            _resident(rwkv_g1.shape[1:]), _resident(rwkv_g2.shape[1:]),
            _resident((1, D_MODEL)), _resident((1, D_MODEL)), _resident((1, D_MODEL)),
            _resident(e_mat.shape), _resident(et_mat.shape), _resident(tri.shape), _resident(sel.shape),
        ],
        out_specs=[slab_spec] * 6 + [
            pl.BlockSpec((1, PRE_ROWS, D_MODEL), lambda b, i: (b, i, 0)),
            pl.BlockSpec((1, N_PAIRS, PRE_ROWS // CHUNK, 1, PAIR), lambda b, i: (b, 0, i, 0, 0)),
        ],
        out_shape=[slab] * 6 + [
            jax.ShapeDtypeStruct((n_b, n_t, D_MODEL), F32),
            jax.ShapeDtypeStruct((n_b, N_PAIRS, n_t // CHUNK, 1, PAIR), F32),
        ],
        compiler_params=pltpu.CompilerParams(
            dimension_semantics=("arbitrary", "arbitrary"), vmem_limit_bytes=VMEM_LIMIT),
        name="rwkv_pre",
    )(x, x, gains, rwkv_mix[0], rwkv_w_rkv[0].astype(BF16),
      row2(rwkv_w0[0]), rwkv_w1[0], rwkv_w2[0], row2(rwkv_a0[0]), rwkv_a1[0], rwkv_a2[0],
      rwkv_g1[0], rwkv_g2[0],
      row2(rwkv_k_k[0]), row2(rwkv_k_a[0]), row2(rwkv_r_k[0]),
      e_mat, et_mat, tri, sel)
    rt, at, kt, bt, vv, bonus, gate, gl = pre_out

    scan_spec = pl.BlockSpec((n_b, SCAN_PAIRS, SCAN_ROWS, PAIR), lambda p, t: (0, p, t, 0))
    n_seq = n_b * SCAN_PAIRS
    y = pl.pallas_call(
        _rwkv_scan_kernel,
        grid=(N_PAIRS // SCAN_PAIRS, n_t // SCAN_ROWS),
        in_specs=[scan_spec] * 6 + [
            pl.BlockSpec((n_b, SCAN_PAIRS, SCAN_ROWS // CHUNK, 1, PAIR), lambda p, t: (0, p, t, 0, 0)),
            pl.BlockSpec((SCAN_PAIRS, 1, PAIR), lambda p, t: (p, 0, 0)),
            pl.BlockSpec((SCAN_PAIRS, 1, PAIR), lambda p, t: (p, 0, 0)),
            _resident(ones_blk.shape),
        ],
        out_specs=scan_spec,
        out_shape=jax.ShapeDtypeStruct((n_b, N_PAIRS, n_t, PAIR), F32),
        scratch_shapes=[pltpu.VMEM((n_seq, PAIR, PAIR), F32),
                        pltpu.VMEM((n_seq, SCAN_ROWS, PAIR), BF16),
                        pltpu.VMEM((n_seq, SCAN_ROWS, PAIR), F32),
                        pltpu.VMEM((n_seq, SCAN_ROWS // CHUNK, PAIR, PAIR), BF16),
                        pltpu.VMEM((n_seq, SCAN_ROWS // CHUNK, PAIR, PAIR), F32)],
        compiler_params=pltpu.CompilerParams(
            dimension_semantics=("arbitrary", "arbitrary"), vmem_limit_bytes=VMEM_LIMIT),
        name="rwkv_scan",
    )(rt, at, kt, bt, vv, bonus, gl,
      rwkv_gn_w[0].reshape(N_PAIRS, 1, PAIR), rwkv_gn_b[0].reshape(N_PAIRS, 1, PAIR), ones_blk)

    n_ffn = n_t // FFN_ROWS
    row_spec = pl.BlockSpec((1, FFN_ROWS, D_MODEL), lambda b, i: (b, i, 0))

    def ffn_specs(layer):
        pick = lambda *_: (layer, 0, 0)
        return [pl.BlockSpec((None,) + shape, pick, pipeline_mode=pl.Buffered(1))
                for shape in ((D_MODEL, D_FF), (D_MODEL, D_FF), (D_FF, D_MODEL))]

    x = pl.pallas_call(
        _mix_ffn_kernel,
        grid=(n_b, n_ffn),
        in_specs=[
            row_spec,
            pl.BlockSpec((1, N_PAIRS, FFN_ROWS, PAIR), lambda b, i: (b, 0, i, 0)),
            row_spec,
            _resident((D_MODEL, D_MODEL)),
            _resident((8, D_MODEL)),
        ] + ffn_specs(0),
        out_specs=row_spec,
        out_shape=jax.ShapeDtypeStruct((n_b, n_t, D_MODEL), F32),
        compiler_params=pltpu.CompilerParams(
            dimension_semantics=("arbitrary", "arbitrary"), vmem_limit_bytes=VMEM_LIMIT),
        name="mix_ffn",
    )(x, y, gate, rwkv_w_o[0].astype(BF16), gains, w_gate, w_up, w_down)

    bs_full = jnp.repeat(sgu_bs[0].T, SGU_BLOCK, axis=1)
    x = pl.pallas_call(
        _sgu_ffn_kernel,
        grid=(n_b, n_ffn),
        in_specs=[
            row_spec,
            _resident((8, D_MODEL)),
            _resident((D_MODEL, 2 * D_MODEL)),
            _resident((1, 2 * D_MODEL)),
            _resident((1, D_MODEL)), _resident((1, D_MODEL)),
            _resident((SGU_GROUPS, SGU_BLOCK, SGU_BLOCK)),
            _resident((SGU_BLOCK, D_MODEL)),
            _resident((D_MODEL, D_MODEL)),
        ] + ffn_specs(1),
        out_specs=row_spec,
        out_shape=jax.ShapeDtypeStruct((n_b, n_t, D_MODEL), F32),
        compiler_params=pltpu.CompilerParams(
            dimension_semantics=("arbitrary", "arbitrary"), vmem_limit_bytes=VMEM_LIMIT),
        name="sgu_ffn",
    )(x, gains, sgu_w_in[0].astype(BF16), row2(sgu_b_in[0]),
      row2(sgu_ln_w[0]), row2(sgu_ln_b[0]), sgu_ws[0], bs_full, sgu_w_out[0].astype(BF16),
      w_gate, w_up, w_down)
    return x
```

```python
import math

import jax
import jax.numpy as jnp
import numpy as np
from jax import lax
from jax.experimental import pallas as pl
from jax.experimental.pallas import tpu as pltpu

F32 = jnp.float32
BF16 = jnp.bfloat16

D_MODEL = 1024
HEAD = 64
PAIR = 2 * HEAD
N_PAIRS = D_MODEL // PAIR
CHUNK = 64
SGU_BLOCK = 128
SGU_GROUPS = 8
D_FF = 2816
RMS_EPS = 1e-6
GN_EPS = 64e-5
LN_EPS = 1e-5

PRE_ROWS = 512
SCAN_ROWS = 512
SCAN_PAIRS = 2
P1_CHUNKS = 2
FFN_ROWS = 512
SUB_ROWS = 256
FFN_SLABS = ((0, 1536), (1536, D_FF))
VMEM_LIMIT = 56 * 1024 * 1024


def _dot(a, b):
    return jnp.dot(a.astype(BF16), b.astype(BF16), preferred_element_type=F32)


def _dot_nt(a, b):
    return lax.dot_general(a.astype(BF16), b.astype(BF16), (((1,), (1,)), ((), ())),
                           preferred_element_type=F32)


def _split2(x):
    hi = x.astype(BF16)
    lo = (x - hi.astype(F32)).astype(BF16)
    return hi, lo


def _rmsnorm(x, g):
    return x * lax.rsqrt(jnp.mean(x * x, axis=-1, keepdims=True) + RMS_EPS) * g


def _softplus(z):
    return jnp.maximum(z, 0.0) + jnp.log(1.0 + jnp.exp(-jnp.abs(z)))


def _gelu(x):
    return x * (lax.erf(x * (1.0 / math.sqrt(2.0))) + 1.0) * 0.5


def _ffn(hs, wg_ref, wu_ref, wd_ref):
    hb = [h.astype(BF16) for h in hs]
    acc = [None] * len(hs)
    for lo, hi in FFN_SLABS:
        sl = slice(lo, hi)
        gate = [_dot(h, wg_ref[:, sl]) for h in hb]
        up = [_dot(h, wu_ref[:, sl]) for h in hb]
        part = [_dot(g * jax.nn.sigmoid(g) * u, wd_ref[sl, :]) for g, u in zip(gate, up)]
        acc = [p if a is None else a + p for a, p in zip(acc, part)]
    return acc


def _sub_tiles(rows):
    return [slice(j, j + SUB_ROWS) for j in range(0, rows, SUB_ROWS)]


def _rwkv_pre_kernel(x_ref, xp_ref, gain_ref, mix_ref, wrkv_ref, w0_ref, w1_ref, w2_ref,
                     a0_ref, a1_ref, a2_ref, g1_ref, g2_ref, kk_ref, ka_ref, rk_ref,
                     e_ref, et_ref, tri_ref, sel_ref,
                     rt_ref, at_ref, kt_ref, bt_ref, v_ref, bonus_ref, g_ref, gl_ref):
    i = pl.program_id(1)
    gain = gain_ref[0:1]
    subs = _sub_tiles(x_ref.shape[1])
    idx = range(len(subs))
    h = [_rmsnorm(x_ref[0, sl, :], gain) for sl in subs]
    hp = _rmsnorm(xp_ref[0], gain)[7:8]
    prev_row = [jnp.where(i > 0, hp, 0.0)] + [hh[SUB_ROWS - 1:SUB_ROWS] for hh in h[:-1]]
    row = lax.broadcasted_iota(jnp.int32, h[0].shape, 0)
    h_b = [hh.astype(BF16) for hh in h]
    xx_b = [(jnp.where(row == 0, pr, pltpu.roll(hh, 1, 0)) - hh).astype(BF16) for hh, pr in zip(h, prev_row)]

    def shifted(c):
        mix_c = mix_ref[c:c + 1, :].astype(BF16)
        return [hb + xb * mix_c for hb, xb in zip(h_b, xx_b)]

    def dots(lhs, w):
        return [_dot(l, w) for l in lhs]

    r = dots(shifted(0), wrkv_ref[0])
    k = dots(shifted(1), wrkv_ref[1])
    v = dots(shifted(2), wrkv_ref[2])
    w_mid = [jnp.tanh(t) for t in dots(shifted(3), w1_ref[...])]
    w_raw = [w0_ref[...] + t for t in dots(w_mid, w2_ref[...])]
    lw = [-jnp.exp(-_softplus(-t) - 0.5) for t in w_raw]
    a = [jax.nn.sigmoid(a0_ref[...] + t) for t in dots(dots(shifted(4), a1_ref[...]), a2_ref[...])]
    g_mid = [jax.nn.sigmoid(t) for t in dots(shifted(5), g1_ref[...])]
    for sl, t in zip(subs, dots(g_mid, g2_ref[...])):
        g_ref[0, sl, :] = t

    def head_sum(qs, parts):
        s = dots(qs, e_ref[...])
        return [sum(_dot(p, et_ref[...]) for p in (_split2(t) if parts == 2 else (t,))) for t in s]

    kkr = [t * kk_ref[...] for t in k]
    ss = head_sum([t * t for t in kkr], 2)
    kk = [t * lax.rsqrt(jnp.maximum(n, 1e-24)) for t, n in zip(kkr, ss)]
    k2 = [kt * (1.0 + (at - 1.0) * ka_ref[...]) for kt, at in zip(k, a)]
    coef = head_sum([rt * kt * rk_ref[...] for rt, kt in zip(r, k2)], 1)
    bvec = [t * at for t, at in zip(kk, a)]

    parts = [_split2(t) for t in lw]
    c = [sum(_dot(tri_ref[...], p) for p in ps) for ps in parts]
    tot = [sum(_dot(sel_ref[...], p) for p in ps)[:SUB_ROWS // CHUNK] for ps in parts]
    e_neg = [jnp.exp(-t) for t in c]
    outs = (
        (rt_ref, [r[j] * jnp.exp(c[j]) for j in idx]),
        (at_ref, [-kk[j] * jnp.exp(c[j] - lw[j]) for j in idx]),
        (kt_ref, [k2[j] * e_neg[j] for j in idx]),
        (bt_ref, [bvec[j] * e_neg[j] for j in idx]),
        (v_ref, v),
        (bonus_ref, [coef[j] * v[j] for j in idx]),
    )
    gl = [jnp.exp(t) for t in tot]
    per_sub = SUB_ROWS // CHUNK
    for p in range(N_PAIRS):
        lanes = slice(p * PAIR, (p + 1) * PAIR)
        for ref, vals in outs:
            for sl, val in zip(subs, vals):
                ref[0, p, sl, :] = val[:, lanes].astype(ref.dtype)
        for j in idx:
            for cj in range(per_sub):
                gl_ref[0, p, j * per_sub + cj] = gl[j][cj:cj + 1, lanes]


def _rwkv_scan_kernel(rt_ref, at_ref, kt_ref, bt_ref, v_ref, bonus_ref, gl_ref,
                      gnw_ref, gnb_ref, ones_ref, y_ref, s_ref, r2_ref, y0_ref, mt_ref, nt_ref, dcol_ref):
    n_batch, n_pp, rows = rt_ref.shape[0], rt_ref.shape[1], rt_ref.shape[2]
    seqs = [(b, q) for b in range(n_batch) for q in range(n_pp)]

    @pl.when(pl.program_id(1) == 0)
    def _():
        s_ref[...] = jnp.zeros_like(s_ref)

    lane = lax.broadcasted_iota(jnp.int32, (CHUNK, PAIR), 1)
    trow = lax.broadcasted_iota(jnp.int32, (CHUNK, PAIR), 0)
    tcol = lane & (HEAD - 1)
    first_head = lane < HEAD
    strict = tcol < trow
    incl = tcol <= trow
    same8, same16, same32 = [(tcol >> s) == (trow >> s) for s in (3, 4, 5)]
    merge_masks = (jnp.logical_and(same16, jnp.logical_not(same8)),
                   jnp.logical_and(same32, jnp.logical_not(same16)),
                   jnp.logical_not(same32))
    eye = jnp.where(tcol == trow, 1.0, 0.0).astype(F32)
    r2 = lax.broadcasted_iota(jnp.int32, (PAIR, PAIR), 0)
    c2 = lax.broadcasted_iota(jnp.int32, (PAIR, PAIR), 1)
    same_head = (r2 >> 6) == (c2 >> 6)

    def bd(y):
        z = jnp.zeros_like(y)
        return jnp.concatenate([jnp.where(first_head, y, z), jnp.where(first_head, z, y)], axis=0)

    def pdot(x, y):
        return _dot(x, bd(y.astype(BF16)))

    cat = jnp.concatenate

    def chunk_rows(ci):
        start = ci * CHUNK
        return pl.ds(start if isinstance(ci, int) else pl.multiple_of(start, CHUNK), CHUNK)

    def phase1(it):
        where = [(b, q, n, it * P1_CHUNKS + cj)
                 for cj in range(P1_CHUNKS) for n, (b, q) in enumerate(seqs)]
        sls = [chunk_rows(ci) for _, _, _, ci in where]
        idx = range(len(where))
        rt = [rt_ref[where[i][0], where[i][1], sls[i], :] for i in idx]
        at = [at_ref[where[i][0], where[i][1], sls[i], :] for i in idx]
        kt = [kt_ref[where[i][0], where[i][1], sls[i], :] for i in idx]
        bt = [bt_ref[where[i][0], where[i][1], sls[i], :] for i in idx]
        vv = [v_ref[where[i][0], where[i][1], sls[i], :] for i in idx]
        bk = [cat([bt[i], kt[i]], axis=0).astype(F32) * gl_ref[where[i][0], where[i][1], where[i][3]]
              for i in idx]
        p_all = [_dot_nt(cat([at[i], rt[i]], axis=0), cat([bd(bt[i]), bd(kt[i])], axis=0)) for i in idx]
        a_ab = [jnp.where(strict, p[:CHUNK, :PAIR], 0.0) for p in p_all]
        a_ak = [jnp.where(strict, p[:CHUNK, PAIR:], 0.0) for p in p_all]
        a_rb = [jnp.where(incl, p[CHUNK:, :PAIR], 0.0) for p in p_all]
        a_rk = [jnp.where(incl, p[CHUNK:, PAIR:], 0.0) for p in p_all]
        av = [_dot(cat([a_ak[i], a_rk[i]], axis=0), bd(vv[i])) for i in idx]
        a_d = [jnp.where(same8, a, 0.0) for a in a_ab]
        t_m = [eye + a for a in a_d]
        p_m = [pdot(a, a) for a in a_d]
        res = [pdot(cat([p_m[i], t_m[i]], axis=0), p_m[i]) for i in idx]
        t_m = [t_m[i] + res[i][CHUNK:] for i in idx]
        t_m = [t_m[i] + pdot(t_m[i], res[i][:CHUNK]) for i in idx]
        for off in merge_masks:
            x_m = [pdot(t_m[i], jnp.where(off, a_ab[i], 0.0)) for i in idx]
            t_m = [t_m[i] + pdot(x_m[i], t_m[i]) for i in idx]
        tx = [_dot(t_m[i], cat([bd(at[i]), bd(av[i][:CHUNK].astype(BF16))], axis=1)) for i in idx]
        at2 = [t[:, :PAIR] for t in tx]
        w_u = [t[:, PAIR:] for t in tx]
        e1 = [_dot(a_rb[i], cat([bd(at2[i].astype(BF16)), bd(w_u[i].astype(BF16))], axis=1)) for i in idx]
        zeros = jnp.zeros((CHUNK, PAIR), F32)
        rhs = [cat([cat([at2[i], zeros], axis=0), cat([w_u[i], vv[i].astype(F32)], axis=0)], axis=1)
               for i in idx]
        e2 = [_dot(bk[i].T, rhs[i]) for i in idx]
        for i in idx:
            b, q, n, ci = where[i]
            r2_ref[n, sls[i], :] = (rt[i].astype(F32) + e1[i][:, :PAIR]).astype(BF16)
            y0_ref[n, sls[i], :] = av[i][CHUNK:] + e1[i][:, PAIR:]
            mt_ref[n, ci] = jnp.where(same_head, e2[i][:, :PAIR], 0.0).astype(BF16)
            nt_ref[n, ci] = jnp.where(same_head, e2[i][:, PAIR:], 0.0)
            dcol_ref[n, ci] = jnp.broadcast_to(gl_ref[b, q, ci], (PAIR, PAIR)).T

    def phase2(it):
        ns = range(len(seqs))
        state = [s_ref[n] for n in ns]
        for cj in range(P1_CHUNKS):
            ci = it * P1_CHUNKS + cj
            sl = chunk_rows(ci)
            out = [_dot(cat([r2_ref[n, sl, :], mt_ref[n, ci]], axis=0), state[n]) for n in ns]
            state = [state[n] * dcol_ref[n, ci] + out[n][CHUNK:] + nt_ref[n, ci] for n in ns]
            for n, (b, q) in enumerate(seqs):
                y_ref[b, q, sl, :] = out[n][:CHUNK] + y0_ref[n, sl, :]
        for n in ns:
            s_ref[n] = state[n]

    n_groups = rows // (CHUNK * P1_CHUNKS)
    phase1(0)
    for it in range(n_groups - 1):
        phase2(it)
        phase1(it + 1)
    phase2(n_groups - 1)

    ones = ones_ref[...]

    def head_mean(z):
        return _dot(z, ones) * (1.0 / HEAD)

    for q in range(n_pp):
        y = y_ref[:, q].reshape(n_batch * rows, PAIR)
        d = y - head_mean(y)
        yn = d * lax.rsqrt(head_mean(d * d) + GN_EPS) * gnw_ref[q] + gnb_ref[q]
        y_ref[:, q] = yn.reshape(n_batch, rows, PAIR) + bonus_ref[:, q]


def _mix_ffn_kernel(x_ref, y_ref, g_ref, wo_ref, gains_ref, wg_ref, wu_ref, wd_ref, o_ref):
    subs = _sub_tiles(x_ref.shape[1])
    y = [jnp.concatenate([y_ref[0, p, sl, :] for p in range(N_PAIRS)], axis=-1) for sl in subs]
    m = [_dot(yy * g_ref[0, sl, :], wo_ref[...]) for yy, sl in zip(y, subs)]
    x1 = [x_ref[0, sl, :] + _rmsnorm(mm, gains_ref[1:2]) for mm, sl in zip(m, subs)]
    f = _ffn([_rmsnorm(xx, gains_ref[2:3]) for xx in x1], wg_ref, wu_ref, wd_ref)
    for sl, xx, ff in zip(subs, x1, f):
        o_ref[0, sl, :] = xx + _rmsnorm(ff, gains_ref[3:4])


def _sgu_ffn_kernel(x_ref, gains_ref, win_ref, bin_ref, lnw_ref, lnb_ref, ws_ref, bs_ref,
                    wout_ref, wg_ref, wu_ref, wd_ref, o_ref):
    subs = _sub_tiles(x_ref.shape[1])
    x = [x_ref[0, sl, :] for sl in subs]
    hb = [_rmsnorm(xx, gains_ref[4:5]).astype(BF16) for xx in x]
    u = [_gelu(_dot(h, win_ref[:, :D_MODEL]) + bin_ref[:, :D_MODEL]) for h in hb]
    v = [_gelu(_dot(h, win_ref[:, D_MODEL:]) + bin_ref[:, D_MODEL:]) for h in hb]

    def layer_norm(vv):
        dv = vv - jnp.mean(vv, axis=-1, keepdims=True)
        var = jnp.mean(dv * dv, axis=-1, keepdims=True)
        return (dv * lax.rsqrt(var + LN_EPS) * lnw_ref[...] + lnb_ref[...]).astype(BF16)

    vn = [layer_norm(vv) for vv in v]
    ri = lax.broadcasted_iota(jnp.int32, (SGU_BLOCK, SGU_BLOCK), 0)
    ci = lax.broadcasted_iota(jnp.int32, (SGU_BLOCK, SGU_BLOCK), 1)
    causal = (ci // CHUNK) <= (ri // CHUNK)
    ws_m = [jnp.where(causal, ws_ref[gi], 0.0).astype(BF16) for gi in range(SGU_GROUPS)]

    def spatial(vv):
        cols = []
        for gi in range(SGU_GROUPS):
            lanes = slice(gi * SGU_BLOCK, (gi + 1) * SGU_BLOCK)
            blocks = [_dot(ws_m[gi], vv[n * SGU_BLOCK:(n + 1) * SGU_BLOCK, lanes]) + bs_ref[:, lanes]
                      for n in range(vv.shape[0] // SGU_BLOCK)]
            cols.append(jnp.concatenate(blocks, axis=0))
        return jnp.concatenate(cols, axis=-1)

    m = [_dot(uu * spatial(vv), wout_ref[...]) for uu, vv in zip(u, vn)]
    x1 = [xx + _rmsnorm(mm, gains_ref[5:6]) for xx, mm in zip(x, m)]
    f = _ffn([_rmsnorm(xx, gains_ref[6:7]) for xx in x1], wg_ref, wu_ref, wd_ref)
    for sl, xx, ff in zip(subs, x1, f):
        o_ref[0, sl, :] = xx + _rmsnorm(ff, gains_ref[7:8])


def _resident(shape):
    zeros = (0,) * len(shape)
    return pl.BlockSpec(shape, lambda *_: zeros, pipeline_mode=pl.Buffered(1))


def kernel(x, norm_gains, rwkv_mix, rwkv_w_rkv, rwkv_w0, rwkv_w1, rwkv_w2, rwkv_a0, rwkv_a1, rwkv_a2, rwkv_g1, rwkv_g2, rwkv_k_k, rwkv_k_a, rwkv_r_k, rwkv_gn_w, rwkv_gn_b, rwkv_w_o, sgu_w_in, sgu_b_in, sgu_ln_w, sgu_ln_b, sgu_ws, sgu_bs, sgu_w_out, ffn_w_gate, ffn_w_up, ffn_w_down):
    n_b, n_t, d = x.shape
    assert d == D_MODEL and n_t % SCAN_ROWS == 0 and n_t % FFN_ROWS == 0 and n_t % PRE_ROWS == 0
    row2 = lambda p: p.reshape(1, -1)

    ch = np.arange(D_MODEL) // HEAD
    e_mat = np.asarray(ch[:, None] == np.arange(PAIR)[None, :], dtype=BF16)
    et_mat = np.ascontiguousarray(e_mat.T)
    tt = np.arange(SUB_ROWS)
    tri = np.asarray((tt[:, None] // CHUNK == tt[None, :] // CHUNK) & (tt[None, :] <= tt[:, None]),
                     dtype=BF16)
    sel = np.asarray(np.arange(8)[:, None] == tt[None, :] // CHUNK, dtype=BF16)
    pl_idx = np.arange(PAIR) // HEAD
    ones_blk = np.asarray(pl_idx[:, None] == pl_idx[None, :], dtype=BF16)
    gains = norm_gains.reshape(-1, D_MODEL)
    w_gate, w_up, w_down = (w.astype(BF16) for w in (ffn_w_gate, ffn_w_up, ffn_w_down))

    n_pre = n_t // PRE_ROWS
    slab = jax.ShapeDtypeStruct((n_b, N_PAIRS, n_t, PAIR), BF16)
    slab_spec = pl.BlockSpec((1, N_PAIRS, PRE_ROWS, PAIR), lambda b, i: (b, 0, i, 0))
    pre_out = pl.pallas_call(
        _rwkv_pre_kernel,
        grid=(n_b, n_pre),
        in_specs=[
            pl.BlockSpec((1, PRE_ROWS, D_MODEL), lambda b, i: (b, i, 0)),
            pl.BlockSpec((1, 8, D_MODEL), lambda b, i: (b, jnp.maximum(i * (PRE_ROWS // 8) - 1, 0), 0)),
            _resident((8, D_MODEL)),
            _resident((6, D_MODEL)),
            _resident((3, D_MODEL, D_MODEL)),
            _resident((1, D_MODEL)), _resident(rwkv_w1.shape[1:]), _resident(rwkv_w2.shape[1:]),
            _resident((1, D_MODEL)), _resident(rwkv_a1.shape[1:]), _resident(rwkv_a2.shape[1:]),
            _resident(rwkv_g1.shape[1:]), _resident(rwkv_g2.shape[1:]),
            _resident((1, D_MODEL)), _resident((1, D_MODEL)), _resident((1, D_MODEL)),
            _resident(e_mat.shape), _resident(et_mat.shape), _resident(tri.shape), _resident(sel.shape),
        ],
        out_specs=[slab_spec] * 6 + [
            pl.BlockSpec((1, PRE_ROWS, D_MODEL), lambda b, i: (b, i, 0)),
            pl.BlockSpec((1, N_PAIRS, PRE_ROWS // CHUNK, 1, PAIR), lambda b, i: (b, 0, i, 0, 0)),
        ],
        out_shape=[slab] * 6 + [
            jax.ShapeDtypeStruct((n_b, n_t, D_MODEL), F32),
            jax.ShapeDtypeStruct((n_b, N_PAIRS, n_t // CHUNK, 1, PAIR), F32),
        ],
        compiler_params=pltpu.CompilerParams(
            dimension_semantics=("arbitrary", "arbitrary"), vmem_limit_bytes=VMEM_LIMIT),
        name="rwkv_pre",
    )(x, x, gains, rwkv_mix[0], rwkv_w_rkv[0].astype(BF16),
      row2(rwkv_w0[0]), rwkv_w1[0], rwkv_w2[0], row2(rwkv_a0[0]), rwkv_a1[0], rwkv_a2[0],
      rwkv_g1[0], rwkv_g2[0],
      row2(rwkv_k_k[0]), row2(rwkv_k_a[0]), row2(rwkv_r_k[0]),
      e_mat, et_mat, tri, sel)
    rt, at, kt, bt, vv, bonus, gate, gl = pre_out

    scan_spec = pl.BlockSpec((n_b, SCAN_PAIRS, SCAN_ROWS, PAIR), lambda p, t: (0, p, t, 0))
    n_seq = n_b * SCAN_PAIRS
    y = pl.pallas_call(
        _rwkv_scan_kernel,
        grid=(N_PAIRS // SCAN_PAIRS, n_t // SCAN_ROWS),
        in_specs=[scan_spec] * 6 + [
            pl.BlockSpec((n_b, SCAN_PAIRS, SCAN_ROWS // CHUNK, 1, PAIR), lambda p, t: (0, p, t, 0, 0)),
            pl.BlockSpec((SCAN_PAIRS, 1, PAIR), lambda p, t: (p, 0, 0)),
            pl.BlockSpec((SCAN_PAIRS, 1, PAIR), lambda p, t: (p, 0, 0)),
            _resident(ones_blk.shape),
        ],
        out_specs=scan_spec,
        out_shape=jax.ShapeDtypeStruct((n_b, N_PAIRS, n_t, PAIR), F32),
        scratch_shapes=[pltpu.VMEM((n_seq, PAIR, PAIR), F32),
                        pltpu.VMEM((n_seq, SCAN_ROWS, PAIR), BF16),
                        pltpu.VMEM((n_seq, SCAN_ROWS, PAIR), F32),
                        pltpu.VMEM((n_seq, SCAN_ROWS // CHUNK, PAIR, PAIR), BF16),
                        pltpu.VMEM((n_seq, SCAN_ROWS // CHUNK, PAIR, PAIR), F32),
                        pltpu.VMEM((n_seq, SCAN_ROWS // CHUNK, PAIR, PAIR), F32)],
        compiler_params=pltpu.CompilerParams(
            dimension_semantics=("arbitrary", "arbitrary"), vmem_limit_bytes=VMEM_LIMIT),
        name="rwkv_scan",
    )(rt, at, kt, bt, vv, bonus, gl,
      rwkv_gn_w[0].reshape(N_PAIRS, 1, PAIR), rwkv_gn_b[0].reshape(N_PAIRS, 1, PAIR), ones_blk)

    n_ffn = n_t // FFN_ROWS
    row_spec = pl.BlockSpec((1, FFN_ROWS, D_MODEL), lambda b, i: (b, i, 0))

    def ffn_specs(layer):
        pick = lambda *_: (layer, 0, 0)
        return [pl.BlockSpec((None,) + shape, pick, pipeline_mode=pl.Buffered(1))
                for shape in ((D_MODEL, D_FF), (D_MODEL, D_FF), (D_FF, D_MODEL))]

    x = pl.pallas_call(
        _mix_ffn_kernel,
        grid=(n_b, n_ffn),
        in_specs=[
            row_spec,
            pl.BlockSpec((1, N_PAIRS, FFN_ROWS, PAIR), lambda b, i: (b, 0, i, 0)),
            row_spec,
            _resident((D_MODEL, D_MODEL)),
            _resident((8, D_MODEL)),
        ] + ffn_specs(0),
        out_specs=row_spec,
        out_shape=jax.ShapeDtypeStruct((n_b, n_t, D_MODEL), F32),
        compiler_params=pltpu.CompilerParams(
            dimension_semantics=("arbitrary", "arbitrary"), vmem_limit_bytes=VMEM_LIMIT),
        name="mix_ffn",
    )(x, y, gate, rwkv_w_o[0].astype(BF16), gains, w_gate, w_up, w_down)

    bs_full = jnp.repeat(sgu_bs[0].T, SGU_BLOCK, axis=1)
    x = pl.pallas_call(
        _sgu_ffn_kernel,
        grid=(n_b, n_ffn),
        in_specs=[
            row_spec,
            _resident((8, D_MODEL)),
            _resident((D_MODEL, 2 * D_MODEL)),
            _resident((1, 2 * D_MODEL)),
            _resident((1, D_MODEL)), _resident((1, D_MODEL)),
            _resident((SGU_GROUPS, SGU_BLOCK, SGU_BLOCK)),
            _resident((SGU_BLOCK, D_MODEL)),
            _resident((D_MODEL, D_MODEL)),
        ] + ffn_specs(1),
        out_specs=row_spec,
        out_shape=jax.ShapeDtypeStruct((n_b, n_t, D_MODEL), F32),
        compiler_params=pltpu.CompilerParams(
            dimension_semantics=("arbitrary", "arbitrary"), vmem_limit_bytes=VMEM_LIMIT),
        name="sgu_ffn",
    )(x, gains, sgu_w_in[0].astype(BF16), row2(sgu_b_in[0]),
      row2(sgu_ln_w[0]), row2(sgu_ln_b[0]), sgu_ws[0], bs_full, sgu_w_out[0].astype(BF16),
      w_gate, w_up, w_down)
    return x
```

```python
import math

import jax
import jax.numpy as jnp
import numpy as np
from jax import lax
from jax.experimental import pallas as pl
from jax.experimental.pallas import tpu as pltpu

F32 = jnp.float32
BF16 = jnp.bfloat16

D_MODEL = 1024
HEAD = 64
PAIR = 2 * HEAD
N_PAIRS = D_MODEL // PAIR
CHUNK = 64
SGU_BLOCK = 128
SGU_GROUPS = 8
D_FF = 2816
RMS_EPS = 1e-6
GN_EPS = 64e-5
LN_EPS = 1e-5

PRE_ROWS = 512
SCAN_ROWS = 512
SCAN_PAIRS = 2
P1_CHUNKS = 2
FFN_ROWS = 512
SUB_ROWS = 256
FFN_SLABS = ((0, 1536), (1536, D_FF))
VMEM_LIMIT = 56 * 1024 * 1024


def _dot(a, b):
    return jnp.dot(a.astype(BF16), b.astype(BF16), preferred_element_type=F32)


def _dot_nt(a, b):
    return lax.dot_general(a.astype(BF16), b.astype(BF16), (((1,), (1,)), ((), ())),
                           preferred_element_type=F32)


def _split2(x):
    hi = x.astype(BF16)
    lo = (x - hi.astype(F32)).astype(BF16)
    return hi, lo


def _rmsnorm(x, g):
    return x * lax.rsqrt(jnp.mean(x * x, axis=-1, keepdims=True) + RMS_EPS) * g


def _softplus(z):
    return jnp.maximum(z, 0.0) + jnp.log(1.0 + jnp.exp(-jnp.abs(z)))


def _gelu(x):
    return x * (lax.erf(x * (1.0 / math.sqrt(2.0))) + 1.0) * 0.5


def _ffn(hs, wg_ref, wu_ref, wd_ref):
    hb = [h.astype(BF16) for h in hs]
    acc = [None] * len(hs)
    for lo, hi in FFN_SLABS:
        sl = slice(lo, hi)
        gate = [_dot(h, wg_ref[:, sl]) for h in hb]
        up = [_dot(h, wu_ref[:, sl]) for h in hb]
        part = [_dot(g * jax.nn.sigmoid(g) * u, wd_ref[sl, :]) for g, u in zip(gate, up)]
        acc = [p if a is None else a + p for a, p in zip(acc, part)]
    return acc


def _sub_tiles(rows):
    return [slice(j, j + SUB_ROWS) for j in range(0, rows, SUB_ROWS)]


def _rwkv_pre_kernel(x_ref, xp_ref, gain_ref, mix_ref, wrkv_ref, w0_ref, w1_ref, w2_ref,
                     a0_ref, a1_ref, a2_ref, g1_ref, g2_ref, kk_ref, ka_ref, rk_ref,
                     e_ref, et_ref, tri_ref, sel_ref,
                     rt_ref, at_ref, kt_ref, bt_ref, v_ref, bonus_ref, g_ref, gl_ref):
    i = pl.program_id(1)
    gain = gain_ref[0:1]
    h = _rmsnorm(x_ref[0], gain)
    hp = _rmsnorm(xp_ref[0], gain)[7:8]
    hp = jnp.where(i > 0, hp, 0.0)
    row = lax.broadcasted_iota(jnp.int32, h.shape, 0)
    h_prev = jnp.where(row == 0, hp, pltpu.roll(h, 1, 0))
    h_b = h.astype(BF16)
    xx_b = (h_prev - h).astype(BF16)

    def shifted(c):
        return h_b + xx_b * mix_ref[c:c + 1, :].astype(BF16)

    r = _dot(shifted(0), wrkv_ref[0])
    k = _dot(shifted(1), wrkv_ref[1])
    v = _dot(shifted(2), wrkv_ref[2])
    w_raw = w0_ref[...] + _dot(jnp.tanh(_dot(shifted(3), w1_ref[...])), w2_ref[...])
    lw = -jnp.exp(-_softplus(-w_raw) - 0.5)
    a = jax.nn.sigmoid(a0_ref[...] + _dot(_dot(shifted(4), a1_ref[...]), a2_ref[...]))
    g_ref[0] = _dot(jax.nn.sigmoid(_dot(shifted(5), g1_ref[...])), g2_ref[...])

    def head_sum(q, parts):
        s = _dot(q, e_ref[...])
        return sum(_dot(p, et_ref[...]) for p in (_split2(s) if parts == 2 else (s,)))

    kkr = k * kk_ref[...]
    ss = head_sum(kkr * kkr, 2)
    kk = kkr * lax.rsqrt(jnp.maximum(ss, 1e-24))
    k2 = k * (1.0 + (a - 1.0) * ka_ref[...])
    bonus = head_sum(r * k2 * rk_ref[...], 1) * v
    avec = -kk
    bvec = kk * a

    parts = _split2(lw)
    n_chunks = h.shape[0] // CHUNK
    blocks = _sub_tiles(h.shape[0])
    c = jnp.concatenate([sum(_dot(tri_ref[...], p[blk]) for p in parts) for blk in blocks], axis=0)
    tot = jnp.concatenate([sum(_dot(sel_ref[...], p[blk]) for p in parts)[:SUB_ROWS // CHUNK]
                           for blk in blocks], axis=0)
    e_neg = jnp.exp(-c)
    outs = (
        (rt_ref, r * jnp.exp(c)),
        (at_ref, avec * jnp.exp(c - lw)),
        (kt_ref, k2 * e_neg),
        (bt_ref, bvec * e_neg),
        (v_ref, v),
        (bonus_ref, bonus),
    )
    gl = jnp.exp(tot)
    for p in range(N_PAIRS):
        sl = slice(p * PAIR, (p + 1) * PAIR)
        for ref, val in outs:
            ref[0, p] = val[:, sl].astype(ref.dtype)
        for j in range(n_chunks):
            gl_ref[0, p, j] = gl[j:j + 1, sl]


def _rwkv_scan_kernel(rt_ref, at_ref, kt_ref, bt_ref, v_ref, bonus_ref, gl_ref,
                      gnw_ref, gnb_ref, ones_ref, y_ref, s_ref, r2_ref, y0_ref, mt_ref, nt_ref, dcol_ref):
    n_batch, n_pp, rows = rt_ref.shape[0], rt_ref.shape[1], rt_ref.shape[2]
    seqs = [(b, q) for b in range(n_batch) for q in range(n_pp)]

    @pl.when(pl.program_id(1) == 0)
    def _():
        s_ref[...] = jnp.zeros_like(s_ref)

    lane = lax.broadcasted_iota(jnp.int32, (CHUNK, PAIR), 1)
    trow = lax.broadcasted_iota(jnp.int32, (CHUNK, PAIR), 0)
    tcol = lane & (HEAD - 1)
    first_head = lane < HEAD
    strict = tcol < trow
    incl = tcol <= trow
    same8, same16, same32 = [(tcol >> s) == (trow >> s) for s in (3, 4, 5)]
    merge_masks = (jnp.logical_and(same16, jnp.logical_not(same8)),
                   jnp.logical_and(same32, jnp.logical_not(same16)),
                   jnp.logical_not(same32))
    eye = jnp.where(tcol == trow, 1.0, 0.0).astype(F32)
    r2 = lax.broadcasted_iota(jnp.int32, (PAIR, PAIR), 0)
    c2 = lax.broadcasted_iota(jnp.int32, (PAIR, PAIR), 1)
    same_head = (r2 >> 6) == (c2 >> 6)

    def bd(y):
        z = jnp.zeros_like(y)
        return jnp.concatenate([jnp.where(first_head, y, z), jnp.where(first_head, z, y)], axis=0)

    def pdot(x, y):
        return _dot(x, bd(y.astype(BF16)))

    cat = jnp.concatenate

    def chunk_rows(ci):
        start = ci * CHUNK
        return pl.ds(start if isinstance(ci, int) else pl.multiple_of(start, CHUNK), CHUNK)

    def phase1(it):
        where = [(b, q, n, it * P1_CHUNKS + cj)
                 for cj in range(P1_CHUNKS) for n, (b, q) in enumerate(seqs)]
        sls = [chunk_rows(ci) for _, _, _, ci in where]
        idx = range(len(where))
        rt = [rt_ref[where[i][0], where[i][1], sls[i], :] for i in idx]
        at = [at_ref[where[i][0], where[i][1], sls[i], :] for i in idx]
        kt = [kt_ref[where[i][0], where[i][1], sls[i], :] for i in idx]
        bt = [bt_ref[where[i][0], where[i][1], sls[i], :] for i in idx]
        vv = [v_ref[where[i][0], where[i][1], sls[i], :] for i in idx]
        bk = [cat([bt[i], kt[i]], axis=0).astype(F32) * gl_ref[where[i][0], where[i][1], where[i][3]]
              for i in idx]
        p_all = [_dot_nt(cat([at[i], rt[i]], axis=0), cat([bd(bt[i]), bd(kt[i])], axis=0)) for i in idx]
        a_ab = [jnp.where(strict, p[:CHUNK, :PAIR], 0.0) for p in p_all]
        a_ak = [jnp.where(strict, p[:CHUNK, PAIR:], 0.0) for p in p_all]
        a_rb = [jnp.where(incl, p[CHUNK:, :PAIR], 0.0) for p in p_all]
        a_rk = [jnp.where(incl, p[CHUNK:, PAIR:], 0.0) for p in p_all]
        av = [_dot(cat([a_ak[i], a_rk[i]], axis=0), bd(vv[i])) for i in idx]
        a_d = [jnp.where(same8, a, 0.0) for a in a_ab]
        t_m = [eye + a for a in a_d]
        p_m = [pdot(a, a) for a in a_d]
        res = [pdot(cat([p_m[i], t_m[i]], axis=0), p_m[i]) for i in idx]
        t_m = [t_m[i] + res[i][CHUNK:] for i in idx]
        t_m = [t_m[i] + pdot(t_m[i], res[i][:CHUNK]) for i in idx]
        for off in merge_masks:
            x_m = [pdot(t_m[i], jnp.where(off, a_ab[i], 0.0)) for i in idx]
            t_m = [t_m[i] + pdot(x_m[i], t_m[i]) for i in idx]
        tx = [_dot(t_m[i], cat([bd(at[i]), bd(av[i][:CHUNK].astype(BF16))], axis=1)) for i in idx]
        at2 = [t[:, :PAIR] for t in tx]
        w_u = [t[:, PAIR:] for t in tx]
        e1 = [_dot(a_rb[i], cat([bd(at2[i].astype(BF16)), bd(w_u[i].astype(BF16))], axis=1)) for i in idx]
        zeros = jnp.zeros((CHUNK, PAIR), F32)
        rhs = [cat([cat([at2[i], zeros], axis=0), cat([w_u[i], vv[i].astype(F32)], axis=0)], axis=1)
               for i in idx]
        e2 = [_dot(bk[i].T, rhs[i]) for i in idx]
        for i in idx:
            b, q, n, ci = where[i]
            r2_ref[n, sls[i], :] = (rt[i].astype(F32) + e1[i][:, :PAIR]).astype(BF16)
            y0_ref[n, sls[i], :] = av[i][CHUNK:] + e1[i][:, PAIR:]
            mt_ref[n, ci] = jnp.where(same_head, e2[i][:, :PAIR], 0.0).astype(BF16)
            nt_ref[n, ci] = jnp.where(same_head, e2[i][:, PAIR:], 0.0)
            dcol_ref[n, ci] = jnp.broadcast_to(gl_ref[b, q, ci], (PAIR, PAIR)).T

    def phase2(it):
        ns = range(len(seqs))
        state = [s_ref[n] for n in ns]
        for cj in range(P1_CHUNKS):
            ci = it * P1_CHUNKS + cj
            sl = chunk_rows(ci)
            out = [_dot(cat([r2_ref[n, sl, :], mt_ref[n, ci]], axis=0), state[n]) for n in ns]
            state = [state[n] * dcol_ref[n, ci] + out[n][CHUNK:] + nt_ref[n, ci] for n in ns]
            for n, (b, q) in enumerate(seqs):
                y_ref[b, q, sl, :] = out[n][:CHUNK] + y0_ref[n, sl, :]
        for n in ns:
            s_ref[n] = state[n]

    n_groups = rows // (CHUNK * P1_CHUNKS)
    phase1(0)
    for it in range(n_groups - 1):
        phase2(it)
        phase1(it + 1)
    phase2(n_groups - 1)

    ones = ones_ref[...]

    def head_mean(z):
        return _dot(z, ones) * (1.0 / HEAD)

    for q in range(n_pp):
        y = y_ref[:, q].reshape(n_batch * rows, PAIR)
        d = y - head_mean(y)
        yn = d * lax.rsqrt(head_mean(d * d) + GN_EPS) * gnw_ref[q] + gnb_ref[q]
        y_ref[:, q] = yn.reshape(n_batch, rows, PAIR) + bonus_ref[:, q]


def _mix_ffn_kernel(x_ref, y_ref, g_ref, wo_ref, gains_ref, wg_ref, wu_ref, wd_ref, o_ref):
    subs = _sub_tiles(x_ref.shape[1])
    y = [jnp.concatenate([y_ref[0, p, sl, :] for p in range(N_PAIRS)], axis=-1) for sl in subs]
    m = [_dot(yy * g_ref[0, sl, :], wo_ref[...]) for yy, sl in zip(y, subs)]
    x1 = [x_ref[0, sl, :] + _rmsnorm(mm, gains_ref[1:2]) for mm, sl in zip(m, subs)]
    f = _ffn([_rmsnorm(xx, gains_ref[2:3]) for xx in x1], wg_ref, wu_ref, wd_ref)
    for sl, xx, ff in zip(subs, x1, f):
        o_ref[0, sl, :] = xx + _rmsnorm(ff, gains_ref[3:4])


def _sgu_ffn_kernel(x_ref, gains_ref, win_ref, bin_ref, lnw_ref, lnb_ref, ws_ref, bs_ref,
                    wout_ref, wg_ref, wu_ref, wd_ref, o_ref):
    subs = _sub_tiles(x_ref.shape[1])
    x = [x_ref[0, sl, :] for sl in subs]
    hb = [_rmsnorm(xx, gains_ref[4:5]).astype(BF16) for xx in x]
    u = [_gelu(_dot(h, win_ref[:, :D_MODEL]) + bin_ref[:, :D_MODEL]) for h in hb]
    v = [_gelu(_dot(h, win_ref[:, D_MODEL:]) + bin_ref[:, D_MODEL:]) for h in hb]

    def layer_norm(vv):
        dv = vv - jnp.mean(vv, axis=-1, keepdims=True)
        var = jnp.mean(dv * dv, axis=-1, keepdims=True)
        return (dv * lax.rsqrt(var + LN_EPS) * lnw_ref[...] + lnb_ref[...]).astype(BF16)

    vn = [layer_norm(vv) for vv in v]
    ri = lax.broadcasted_iota(jnp.int32, (SGU_BLOCK, SGU_BLOCK), 0)
    ci = lax.broadcasted_iota(jnp.int32, (SGU_BLOCK, SGU_BLOCK), 1)
    causal = (ci // CHUNK) <= (ri // CHUNK)
    ws_m = [jnp.where(causal, ws_ref[gi], 0.0).astype(BF16) for gi in range(SGU_GROUPS)]

    def spatial(vv):
        cols = []
        for gi in range(SGU_GROUPS):
            lanes = slice(gi * SGU_BLOCK, (gi + 1) * SGU_BLOCK)
            blocks = [_dot(ws_m[gi], vv[n * SGU_BLOCK:(n + 1) * SGU_BLOCK, lanes]) + bs_ref[:, lanes]
                      for n in range(vv.shape[0] // SGU_BLOCK)]
            cols.append(jnp.concatenate(blocks, axis=0))
        return jnp.concatenate(cols, axis=-1)

    m = [_dot(uu * spatial(vv), wout_ref[...]) for uu, vv in zip(u, vn)]
    x1 = [xx + _rmsnorm(mm, gains_ref[5:6]) for xx, mm in zip(x, m)]
    f = _ffn([_rmsnorm(xx, gains_ref[6:7]) for xx in x1], wg_ref, wu_ref, wd_ref)
    for sl, xx, ff in zip(subs, x1, f):
        o_ref[0, sl, :] = xx + _rmsnorm(ff, gains_ref[7:8])


def _resident(shape):
    zeros = (0,) * len(shape)
    return pl.BlockSpec(shape, lambda *_: zeros, pipeline_mode=pl.Buffered(1))


def kernel(x, norm_gains, rwkv_mix, rwkv_w_rkv, rwkv_w0, rwkv_w1, rwkv_w2, rwkv_a0, rwkv_a1, rwkv_a2, rwkv_g1, rwkv_g2, rwkv_k_k, rwkv_k_a, rwkv_r_k, rwkv_gn_w, rwkv_gn_b, rwkv_w_o, sgu_w_in, sgu_b_in, sgu_ln_w, sgu_ln_b, sgu_ws, sgu_bs, sgu_w_out, ffn_w_gate, ffn_w_up, ffn_w_down):
    n_b, n_t, d = x.shape
    assert d == D_MODEL and n_t % SCAN_ROWS == 0 and n_t % FFN_ROWS == 0 and n_t % PRE_ROWS == 0
    row2 = lambda p: p.reshape(1, -1)

    ch = np.arange(D_MODEL) // HEAD
    e_mat = np.asarray(ch[:, None] == np.arange(PAIR)[None, :], dtype=BF16)
    et_mat = np.ascontiguousarray(e_mat.T)
    tt = np.arange(SUB_ROWS)
    tri = np.asarray((tt[:, None] // CHUNK == tt[None, :] // CHUNK) & (tt[None, :] <= tt[:, None]),
                     dtype=BF16)
    sel = np.asarray(np.arange(8)[:, None] == tt[None, :] // CHUNK, dtype=BF16)
    pl_idx = np.arange(PAIR) // HEAD
    ones_blk = np.asarray(pl_idx[:, None] == pl_idx[None, :], dtype=BF16)
    gains = norm_gains.reshape(-1, D_MODEL)
    w_gate, w_up, w_down = (w.astype(BF16) for w in (ffn_w_gate, ffn_w_up, ffn_w_down))

    n_pre = n_t // PRE_ROWS
    slab = jax.ShapeDtypeStruct((n_b, N_PAIRS, n_t, PAIR), BF16)
    slab_spec = pl.BlockSpec((1, N_PAIRS, PRE_ROWS, PAIR), lambda b, i: (b, 0, i, 0))
    pre_out = pl.pallas_call(
        _rwkv_pre_kernel,
        grid=(n_b, n_pre),
        in_specs=[
            pl.BlockSpec((1, PRE_ROWS, D_MODEL), lambda b, i: (b, i, 0)),
            pl.BlockSpec((1, 8, D_MODEL), lambda b, i: (b, jnp.maximum(i * (PRE_ROWS // 8) - 1, 0), 0)),
            _resident((8, D_MODEL)),
            _resident((6, D_MODEL)),
            _resident((3, D_MODEL, D_MODEL)),
            _resident((1, D_MODEL)), _resident(rwkv_w1.shape[1:]), _resident(rwkv_w2.shape[1:]),
            _resident((1, D_MODEL)), _resident(rwkv_a1.shape[1:]), _resident(rwkv_a2.shape[1:]),
            _resident(rwkv_g1.shape[1:]), _resident(rwkv_g2.shape[1:]),
            _resident((1, D_MODEL)), _resident((1, D_MODEL)), _resident((1, D_MODEL)),
            _resident(e_mat.shape), _resident(et_mat.shape), _resident(tri.shape), _resident(sel.shape),
        ],
        out_specs=[slab_spec] * 6 + [
            pl.BlockSpec((1, PRE_ROWS, D_MODEL), lambda b, i: (b, i, 0)),
            pl.BlockSpec((1, N_PAIRS, PRE_ROWS // CHUNK, 1, PAIR), lambda b, i: (b, 0, i, 0, 0)),
        ],
        out_shape=[slab] * 6 + [
            jax.ShapeDtypeStruct((n_b, n_t, D_MODEL), F32),
            jax.ShapeDtypeStruct((n_b, N_PAIRS, n_t // CHUNK, 1, PAIR), F32),
        ],
        compiler_params=pltpu.CompilerParams(
            dimension_semantics=("arbitrary", "arbitrary"), vmem_limit_bytes=VMEM_LIMIT),
        name="rwkv_pre",
    )(x, x, gains, rwkv_mix[0], rwkv_w_rkv[0].astype(BF16),
      row2(rwkv_w0[0]), rwkv_w1[0], rwkv_w2[0], row2(rwkv_a0[0]), rwkv_a1[0], rwkv_a2[0],
      rwkv_g1[0], rwkv_g2[0],
      row2(rwkv_k_k[0]), row2(rwkv_k_a[0]), row2(rwkv_r_k[0]),
      e_mat, et_mat, tri, sel)
    rt, at, kt, bt, vv, bonus, gate, gl = pre_out

    scan_spec = pl.BlockSpec((n_b, SCAN_PAIRS, SCAN_ROWS, PAIR), lambda p, t: (0, p, t, 0))
    n_seq = n_b * SCAN_PAIRS
    y = pl.pallas_call(
        _rwkv_scan_kernel,
        grid=(N_PAIRS // SCAN_PAIRS, n_t // SCAN_ROWS),
        in_specs=[scan_spec] * 6 + [
            pl.BlockSpec((n_b, SCAN_PAIRS, SCAN_ROWS // CHUNK, 1, PAIR), lambda p, t: (0, p, t, 0, 0)),
            pl.BlockSpec((SCAN_PAIRS, 1, PAIR), lambda p, t: (p, 0, 0)),
            pl.BlockSpec((SCAN_PAIRS, 1, PAIR), lambda p, t: (p, 0, 0)),
            _resident(ones_blk.shape),
        ],
        out_specs=scan_spec,
        out_shape=jax.ShapeDtypeStruct((n_b, N_PAIRS, n_t, PAIR), F32),
        scratch_shapes=[pltpu.VMEM((n_seq, PAIR, PAIR), F32),
                        pltpu.VMEM((n_seq, SCAN_ROWS, PAIR), BF16),
                        pltpu.VMEM((n_seq, SCAN_ROWS, PAIR), F32),
                        pltpu.VMEM((n_seq, SCAN_ROWS // CHUNK, PAIR, PAIR), BF16),
                        pltpu.VMEM((n_seq, SCAN_ROWS // CHUNK, PAIR, PAIR), F32),
                        pltpu.VMEM((n_seq, SCAN_ROWS // CHUNK, PAIR, PAIR), F32)],
        compiler_params=pltpu.CompilerParams(
            dimension_semantics=("arbitrary", "arbitrary"), vmem_limit_bytes=VMEM_LIMIT),
        name="rwkv_scan",
    )(rt, at, kt, bt, vv, bonus, gl,
      rwkv_gn_w[0].reshape(N_PAIRS, 1, PAIR), rwkv_gn_b[0].reshape(N_PAIRS, 1, PAIR), ones_blk)

    n_ffn = n_t // FFN_ROWS
    row_spec = pl.BlockSpec((1, FFN_ROWS, D_MODEL), lambda b, i: (b, i, 0))

    def ffn_specs(layer):
        pick = lambda *_: (layer, 0, 0)
        return [pl.BlockSpec((None,) + shape, pick, pipeline_mode=pl.Buffered(1))
                for shape in ((D_MODEL, D_FF), (D_MODEL, D_FF), (D_FF, D_MODEL))]

    x = pl.pallas_call(
        _mix_ffn_kernel,
        grid=(n_b, n_ffn),
        in_specs=[
            row_spec,
            pl.BlockSpec((1, N_PAIRS, FFN_ROWS, PAIR), lambda b, i: (b, 0, i, 0)),
            row_spec,
            _resident((D_MODEL, D_MODEL)),
            _resident((8, D_MODEL)),
        ] + ffn_specs(0),
        out_specs=row_spec,
        out_shape=jax.ShapeDtypeStruct((n_b, n_t, D_MODEL), F32),
        compiler_params=pltpu.CompilerParams(
            dimension_semantics=("arbitrary", "arbitrary"), vmem_limit_bytes=VMEM_LIMIT),
        name="mix_ffn",
    )(x, y, gate, rwkv_w_o[0].astype(BF16), gains, w_gate, w_up, w_down)

    bs_full = jnp.repeat(sgu_bs[0].T, SGU_BLOCK, axis=1)
    x = pl.pallas_call(
        _sgu_ffn_kernel,
        grid=(n_b, n_ffn),
        in_specs=[
            row_spec,
            _resident((8, D_MODEL)),
            _resident((D_MODEL, 2 * D_MODEL)),
            _resident((1, 2 * D_MODEL)),
            _resident((1, D_MODEL)), _resident((1, D_MODEL)),
            _resident((SGU_GROUPS, SGU_BLOCK, SGU_BLOCK)),
            _resident((SGU_BLOCK, D_MODEL)),
            _resident((D_MODEL, D_MODEL)),
        ] + ffn_specs(1),
        out_specs=row_spec,
        out_shape=jax.ShapeDtypeStruct((n_b, n_t, D_MODEL), F32),
        compiler_params=pltpu.CompilerParams(
            dimension_semantics=("arbitrary", "arbitrary"), vmem_limit_bytes=VMEM_LIMIT),
        name="sgu_ffn",
    )(x, gains, sgu_w_in[0].astype(BF16), row2(sgu_b_in[0]),
      row2(sgu_ln_w[0]), row2(sgu_ln_b[0]), sgu_ws[0], bs_full, sgu_w_out[0].astype(BF16),
      w_gate, w_up, w_down)
    return x
```
